```python
import jax
import jax.numpy as jnp
from jax import lax
import numpy as np

D_MODEL = 1024
BATCH = 16
SEQ = 2048
DEPTH = 2

CHUNK = 64
CONV_WIDTH = 4
EPS = 1e-6

MLSTM_HEADS = 4
MLSTM_DQK = 128
MLSTM_DV = 256
MLSTM_QK_W = MLSTM_HEADS * MLSTM_DQK
MLSTM_V_W = MLSTM_HEADS * MLSTM_DV
GDN_HEADS = 8
GDN_DK = 128
GDN_DV = 128
GDN_QK_W = GDN_HEADS * GDN_DK
GDN_V_W = GDN_HEADS * GDN_DV
GDN_QKV_W = 2 * GDN_QK_W + GDN_V_W
ATT_HEADS = 16
ATT_DH = 64
ATT_W = ATT_HEADS * ATT_DH
ATT_LEFT_CHUNKS = 8
REL_MAX_PAST = 256
REL_TABLE = CHUNK + REL_MAX_PAST
SSD_HEADS = 32
SSD_DH = 64
SSD_STATE = 128
SSD_GROUPS = 4
SSD_HEADS_PER_GROUP = SSD_HEADS // SSD_GROUPS
SSD_INNER = SSD_HEADS * SSD_DH
SSD_BC_W = SSD_GROUPS * SSD_STATE
SSD_CONV_W = SSD_INNER + 2 * SSD_BC_W
MOE_GROUPS = 4
MOE_EXPERTS_PER_GROUP = 8
MOE_TOP_K = 2
N_EXPERTS = MOE_GROUPS * MOE_EXPERTS_PER_GROUP
D_EXPERT = 256

EVEN_SIZES = (MLSTM_QK_W, MLSTM_QK_W, MLSTM_V_W, MLSTM_V_W, MLSTM_HEADS, MLSTM_HEADS,
              GDN_QKV_W, GDN_V_W, GDN_HEADS, GDN_HEADS)
EVEN_IN = sum(EVEN_SIZES)
EVEN_OUT = MLSTM_V_W + GDN_V_W
ODD_SIZES = (ATT_W, ATT_W, ATT_W, SSD_INNER, SSD_CONV_W, SSD_HEADS)
ODD_IN = sum(ODD_SIZES)
ODD_OUT = ATT_W + SSD_INNER

kernel_name = "hybrid_chunk_causal_mlstm_gdn_attn_ssd_hmoe"


def _split(x, sizes):
    outs, start = [], 0
    for size in sizes:
        outs.append(x[..., start:start + size])
        start += size
    return outs


def rms_normalize(x):
    xf = x.astype(jnp.float32)
    return xf * lax.rsqrt(jnp.mean(xf * xf, axis=-1, keepdims=True) + EPS)


def rms_norm(x, gain):
    return (rms_normalize(x) * gain).astype(x.dtype)


def l2_normalize(x):
    return x * lax.rsqrt(jnp.sum(x * x, axis=-1, keepdims=True) + EPS)


def causal_depthwise_conv(x, w):
    seq = x.shape[1]
    xp = jnp.pad(x, ((0, 0), (CONV_WIDTH - 1, 0), (0, 0)))
    y = xp[:, 0:seq] * w[0]
    for j in range(1, CONV_WIDTH):
        y = y + xp[:, j:j + seq] * w[j]
    return y


def _chunks_head_major(t, n_chunks):
    b, _, h = t.shape[:3]
    t = t.reshape((b, n_chunks, CHUNK, h) + t.shape[3:])
    return jnp.swapaxes(t, 2, 3)


def _causal_mask(offset=0):
    return jnp.tril(jnp.ones((CHUNK, CHUNK), dtype=bool), offset)


def mlstm_chunkwise(q, k, v, i_pre, f_pre):
    b_, s_, h_, dk = q.shape
    dv = v.shape[-1]
    nc = s_ // CHUNK
    q, k, v, li, fp = [_chunks_head_major(t, nc) for t in (q, k, v, i_pre, f_pre)]
    q = q * dk ** -0.5
    bcum = jnp.cumsum(jax.nn.log_sigmoid(fp), axis=-1)
    b_end = bcum[..., -1]
    g_end = b_end[..., None] - bcum + li
    m_loc = jnp.max(g_end, axis=-1)
    w_end = jnp.exp(g_end - m_loc[..., None])
    upd_c = jnp.einsum('bchs,bchsk,bchsv->bchkv', w_end, k, v)
    upd_n = jnp.einsum('bchs,bchsk->bchk', w_end, k)

    def step(carry, xs):
        c_st, n_st, m_st = carry
        a, ml, uc, un = xs
        m_new = jnp.maximum(a + m_st, ml)
        s_old = jnp.exp(a + m_st - m_new)
        s_upd = jnp.exp(ml - m_new)
        c_new = s_old[..., None, None] * c_st + s_upd[..., None, None] * uc
        n_new = s_old[..., None] * n_st + s_upd[..., None] * un
        return (c_new, n_new, m_new), (c_st, n_st, m_st)

    init = (jnp.zeros((b_, h_, dk, dv), q.dtype), jnp.zeros((b_, h_, dk), q.dtype),
            jnp.zeros((b_, h_), q.dtype))
    xs = tuple(jnp.moveaxis(t, 1, 0) for t in (b_end, m_loc, upd_c, upd_n))
    _, (c_prev, n_prev, m_prev) = lax.scan(step, init, xs)
    c_prev, n_prev, m_prev = [jnp.moveaxis(t, 0, 1) for t in (c_prev, n_prev, m_prev)]

    log_d = jnp.where(_causal_mask(), bcum[..., :, None] - bcum[..., None, :] + li[..., None, :], -jnp.inf)
    inter_log = bcum + m_prev[..., None]
    m_t = jnp.maximum(inter_log, jnp.max(log_d, axis=-1))
    w_intra = jnp.exp(log_d - m_t[..., None])
    w_inter = jnp.exp(inter_log - m_t)
    scores = jnp.einsum('bchtk,bchsk->bchts', q, k) * w_intra
    num = (jnp.einsum('bchts,bchsv->bchtv', scores, v)
           + w_inter[..., None] * jnp.einsum('bchtk,bchkv->bchtv', q, c_prev))
    den = jnp.sum(scores, axis=-1) + w_inter * jnp.einsum('bchtk,bchk->bcht', q, n_prev)
    h = num / jnp.maximum(jnp.abs(den), jnp.exp(-m_t))[..., None]
    return jnp.swapaxes(h, 2, 3).reshape(b_, s_, h_, dv)


def gated_delta_chunkwise(q, k, v, g, beta):
    b_, s_, h_, dk = q.shape
    dv = v.shape[-1]
    nc = s_ // CHUNK
    q, k, v, g, beta = [_chunks_head_major(t, nc) for t in (q, k, v, g, beta)]
    gcum = jnp.cumsum(g, axis=-1)
    decay = jnp.exp(jnp.where(_causal_mask(), gcum[..., :, None] - gcum[..., None, :], -jnp.inf))
    kk = jnp.einsum('bchtk,bchsk->bchts', k, k)
    a_mat = jnp.where(_causal_mask(-1), beta[..., :, None] * kk * decay, 0.0)
    rhs = jnp.concatenate([beta[..., None] * v, (beta * jnp.exp(gcum))[..., None] * k], axis=-1)
    sol = lax.linalg.triangular_solve(a_mat + jnp.eye(CHUNK, dtype=a_mat.dtype), rhs,
                                      left_side=True, lower=True, unit_diagonal=True)
    u, w = sol[..., :dv], sol[..., dv:]
    qk = jnp.einsum('bchtk,bchsk->bchts', q, k) * decay
    q_dec = q * jnp.exp(gcum)[..., None]
    k_dec = k * jnp.exp(gcum[..., -1:] - gcum)[..., None]
    g_tot = jnp.exp(gcum[..., -1])

    def step(s_st, xs):
        u_c, w_c, qk_c, qd_c, kd_c, gt_c = xs
        v_new = u_c - jnp.einsum('bhlk,bhkv->bhlv', w_c, s_st)
        o = jnp.einsum('bhlk,bhkv->bhlv', qd_c, s_st) + jnp.einsum('bhts,bhsv->bhtv', qk_c, v_new)
        s_new = gt_c[..., None, None] * s_st + jnp.einsum('bhlk,bhlv->bhkv', kd_c, v_new)
        return s_new, o

    xs = tuple(jnp.moveaxis(t, 1, 0) for t in (u, w, qk, q_dec, k_dec, g_tot))
    _, o = lax.scan(step, jnp.zeros((b_, h_, dk, dv), q.dtype), xs)
    o = jnp.moveaxis(o, 0, 1)
    return jnp.swapaxes(o, 2, 3).reshape(b_, s_, h_, dv)


def chunked_band_attention(q, k, v, rel_bias):
    b_, s_, h_, dh = q.shape
    nc = s_ // CHUNK
    pad = ATT_LEFT_CHUNKS * CHUNK
    band = pad + CHUNK
    k_pad = jnp.pad(k, ((0, 0), (pad, 0), (0, 0), (0, 0)))
    v_pad = jnp.pad(v, ((0, 0), (pad, 0), (0, 0), (0, 0)))
    rel = jnp.arange(CHUNK)[:, None] + pad - jnp.arange(band)[None, :]
    rel_idx = jnp.clip(rel, -(CHUNK - 1), REL_MAX_PAST) + (CHUNK - 1)
    bias = rel_bias.astype(jnp.float32)[:, rel_idx]
    q_chunks = jnp.moveaxis(q.reshape(b_, nc, CHUNK, h_, dh), 1, 0)

    def one_chunk(args):
        qc, c = args
        kc = lax.dynamic_slice_in_dim(k_pad, c * CHUNK, band, axis=1)
        vc = lax.dynamic_slice_in_dim(v_pad, c * CHUNK, band, axis=1)
        s = jnp.einsum('blhd,bjhd->bhlj', qc, kc).astype(jnp.float32) + bias
        valid = (c * CHUNK - pad + jnp.arange(band)) >= 0
        p = jax.nn.softmax(jnp.where(valid, s, -jnp.inf), axis=-1).astype(vc.dtype)
        return jnp.einsum('bhlj,bjhd->blhd', p, vc)

    out = lax.map(one_chunk, (q_chunks, jnp.arange(nc)))
    return jnp.moveaxis(out, 0, 1).reshape(b_, s_, h_, dh)


def ssd_chunkwise(x, dt, a_head, bm, cm):
    b_, s_, g_, j_, p_ = x.shape
    n_ = bm.shape[-1]
    nc = s_ // CHUNK
    xd = (x * dt[..., None]).reshape(b_, nc, CHUNK, g_, j_, p_)
    acum = jnp.cumsum((dt * a_head).reshape(b_, nc, CHUNK, g_, j_), axis=2)
    bc = bm.reshape(b_, nc, CHUNK, g_, n_)
    cc = cm.reshape(b_, nc, CHUNK, g_, n_)
    seg = acum[:, :, :, None] - acum[:, :, None, :]
    decay = jnp.exp(jnp.where(_causal_mask()[:, :, None, None], seg, -jnp.inf))
    cb = jnp.einsum('bctgn,bcsgn->bctsg', cc, bc)
    y_diag = jnp.einsum('bctsgj,bcsgjp->bctgjp', cb[..., None] * decay, xd)
    decay_end = jnp.exp(acum[:, :, -1:] - acum)
    states = jnp.einsum('bclgn,bclgjp->bcgjpn', bc, decay_end[..., None] * xd)
    chunk_decay = jnp.exp(acum[:, :, -1])

    def step(s_st, xs):
        st, dec = xs
        return dec[..., None, None] * s_st + st, s_st

    _, prev = lax.scan(step, jnp.zeros((b_, g_, j_, p_, n_), xd.dtype),
                       (jnp.moveaxis(states, 1, 0), jnp.moveaxis(chunk_decay, 1, 0)))
    prev = jnp.moveaxis(prev, 0, 1)
    y_off = jnp.einsum('bctgn,bcgjpn->bctgjp', cc, prev) * jnp.exp(acum)[..., None]
    return (y_diag + y_off).reshape(b_, s_, g_, j_, p_)


def even_mixer(h, w_in, i_bias, f_bias, m_norm, conv_w, a_log, dt_bias, g_norm, w_out):
    b_, s_, _ = h.shape
    proj = jnp.einsum('bsd,de->bse', h, w_in).astype(jnp.float32)
    mq, mk, mv, mo, mi, mf, qkv, gz, ga, gb = _split(proj, EVEN_SIZES)
    hm = mlstm_chunkwise(mq.reshape(b_, s_, MLSTM_HEADS, MLSTM_DQK),
                         mk.reshape(b_, s_, MLSTM_HEADS, MLSTM_DQK),
                         mv.reshape(b_, s_, MLSTM_HEADS, MLSTM_DV),
                         mi + i_bias, mf + f_bias)
    hm = (rms_normalize(hm) * m_norm.reshape(MLSTM_HEADS, MLSTM_DV)).reshape(b_, s_, MLSTM_V_W)
    hm = hm * jax.nn.sigmoid(mo)
    qkv = jax.nn.silu(causal_depthwise_conv(qkv, conv_w))
    gq, gk, gv = _split(qkv, (GDN_QK_W, GDN_QK_W, GDN_V_W))
    gq = l2_normalize(gq.reshape(b_, s_, GDN_HEADS, GDN_DK)) * GDN_DK ** -0.5
    gk = l2_normalize(gk.reshape(b_, s_, GDN_HEADS, GDN_DK))
    log_decay = -jnp.exp(a_log) * jax.nn.softplus(ga + dt_bias)
    o = gated_delta_chunkwise(gq, gk, gv.reshape(b_, s_, GDN_HEADS, GDN_DV), log_decay, jax.nn.sigmoid(gb))
    o = (rms_normalize(o) * g_norm).reshape(b_, s_, GDN_V_W) * jax.nn.silu(gz)
    return jnp.einsum('bse,ed->bsd', jnp.concatenate([hm, o], axis=-1), w_out)


def odd_mixer(h, w_in, q_gain, k_gain, rel_bias, conv_w, conv_b, a_log, dt_bias, d_skip, s_norm, w_out):
    b_, s_, _ = h.shape
    proj = jnp.einsum('bsd,de->bse', h, w_in).astype(jnp.float32)
    aq, ak, av, z, xbc, dt_pre = _split(proj, ODD_SIZES)
    q = rms_normalize(aq.reshape(b_, s_, ATT_HEADS, ATT_DH)) * q_gain * ATT_DH ** -0.5
    k = rms_normalize(ak.reshape(b_, s_, ATT_HEADS, ATT_DH)) * k_gain
    attn = chunked_band_attention(q, k, av.reshape(b_, s_, ATT_HEADS, ATT_DH), rel_bias)
    attn = attn.reshape(b_, s_, ATT_W)
    xbc = jax.nn.silu(causal_depthwise_conv(xbc, conv_w) + conv_b)
    xs, bm, cm = _split(xbc, (SSD_INNER, SSD_BC_W, SSD_BC_W))
    dt = jax.nn.softplus(dt_pre + dt_bias).reshape(b_, s_, SSD_GROUPS, SSD_HEADS_PER_GROUP)
    a_head = (-jnp.exp(a_log)).reshape(SSD_GROUPS, SSD_HEADS_PER_GROUP)
    xh = xs.reshape(b_, s_, SSD_GROUPS, SSD_HEADS_PER_GROUP, SSD_DH)
    y = ssd_chunkwise(xh, dt, a_head, bm.reshape(b_, s_, SSD_GROUPS, SSD_STATE),
                      cm.reshape(b_, s_, SSD_GROUPS, SSD_STATE))
    y = y + d_skip.reshape(SSD_GROUPS, SSD_HEADS_PER_GROUP)[..., None] * xh
    y = y.reshape(b_, s_, SSD_INNER) * jax.nn.silu(z)
    y = rms_normalize(y.reshape(b_, s_, SSD_GROUPS, SSD_INNER // SSD_GROUPS)).reshape(b_, s_, SSD_INNER) * s_norm
    return jnp.einsum('bse,ed->bsd', jnp.concatenate([attn, y], axis=-1), w_out)


def hierarchical_moe(h, wg, bg, we, be, w_gate, w_up, w_down):
    b_, s_, _ = h.shape
    hf = h.astype(jnp.float32)
    grp_logits = jnp.einsum('bsd,dg->bsg', hf, wg.astype(jnp.float32)) + bg.astype(jnp.float32)
    grp_prob = jax.nn.softmax(grp_logits, axis=-1)
    grp_idx = jnp.argmax(grp_logits, axis=-1)
    grp_w = jnp.take_along_axis(grp_prob, grp_idx[..., None], axis=-1)
    exp_logits = (jnp.einsum('bsd,de->bse', hf, we.astype(jnp.float32)) + be.astype(jnp.float32))
    exp_logits = exp_logits.reshape(b_, s_, MOE_GROUPS, MOE_EXPERTS_PER_GROUP)
    in_grp = jnp.take_along_axis(exp_logits, grp_idx[:, :, None, None], axis=2)[:, :, 0]
    top_val, top_idx = lax.top_k(in_grp, MOE_TOP_K)
    top_w = jax.nn.softmax(top_val, axis=-1)
    within = jnp.sum(jax.nn.one_hot(top_idx, MOE_EXPERTS_PER_GROUP, dtype=jnp.float32) * top_w[..., None], axis=-2)
    combine = (jax.nn.one_hot(grp_idx, MOE_GROUPS, dtype=jnp.float32)[..., None]
               * (grp_w * within)[:, :, None, :])
    out = jnp.zeros(h.shape, jnp.float32)
    for g in range(MOE_GROUPS):
        a = jnp.einsum('bsd,edh->bseh', h, w_gate[g])
        u = jnp.einsum('bsd,edh->bseh', h, w_up[g])
        act = jax.nn.silu(a) * u * combine[:, :, g, :, None]
        out = out + jnp.einsum('bseh,ehd->bsd', act, w_down[g])
    return out


def setup_inputs(seed: int = 0) -> dict:
    key = jax.random.key(seed)
    ks = jax.random.split(key, 32)
    f32 = jnp.float32
    n_even = (DEPTH + 1) // 2
    n_odd = DEPTH // 2

    def nrm(k, shape, scale):
        return jax.random.normal(k, shape, f32) * scale

    def gain(k, shape):
        return 1.0 + 0.02 * jax.random.normal(k, shape, f32)

    def a_log_init(k, shape):
        return jnp.log(jax.random.uniform(k, shape, f32, 1.0, 16.0))

    def dt_bias_init(k, shape):
        dt = jnp.exp(jax.random.uniform(k, shape, f32, float(np.log(1e-3)), float(np.log(1e-1))))
        return dt + jnp.log(-jnp.expm1(-dt))

    return {
        "x": nrm(ks[0], (BATCH, SEQ, D_MODEL), 1.0),
        "norm_mix": gain(ks[1], (DEPTH, D_MODEL)),
        "norm_ffn": gain(ks[2], (DEPTH, D_MODEL)),
        "even_w_in": nrm(ks[3], (n_even, D_MODEL, EVEN_IN), D_MODEL ** -0.5),
        "mlstm_i_bias": nrm(ks[4], (n_even, MLSTM_HEADS), 0.1),
        "mlstm_f_bias": jnp.linspace(3.0, 6.0, MLSTM_HEADS, dtype=f32)[None] + nrm(ks[5], (n_even, MLSTM_HEADS), 0.1),
        "mlstm_norm": gain(ks[6], (n_even, MLSTM_V_W)),
        "gdn_conv": nrm(ks[7], (n_even, CONV_WIDTH, GDN_QKV_W), CONV_WIDTH ** -0.5),
        "gdn_A_log": a_log_init(ks[8], (n_even, GDN_HEADS)),
        "gdn_dt_bias": dt_bias_init(ks[9], (n_even, GDN_HEADS)),
        "gdn_norm": gain(ks[10], (n_even, GDN_DV)),
        "even_w_out": nrm(ks[11], (n_even, EVEN_OUT, D_MODEL), EVEN_OUT ** -0.5),
        "odd_w_in": nrm(ks[12], (n_odd, D_MODEL, ODD_IN), D_MODEL ** -0.5),
        "att_q_norm": gain(ks[13], (n_odd, ATT_DH)),
        "att_k_norm": gain(ks[14], (n_odd, ATT_DH)),
        "att_rel_bias": nrm(ks[15], (n_odd, ATT_HEADS, REL_TABLE), 0.1),
        "ssd_conv": nrm(ks[16], (n_odd, CONV_WIDTH, SSD_CONV_W), CONV_WIDTH ** -0.5),
        "ssd_conv_bias": nrm(ks[17], (n_odd, SSD_CONV_W), 0.02),
        "ssd_A_log": a_log_init(ks[18], (n_odd, SSD_HEADS)),
        "ssd_dt_bias": dt_bias_init(ks[19], (n_odd, SSD_HEADS)),
        "ssd_D": gain(ks[20], (n_odd, SSD_HEADS)),
        "ssd_norm": gain(ks[21], (n_odd, SSD_INNER)),
        "odd_w_out": nrm(ks[22], (n_odd, ODD_OUT, D_MODEL), ODD_OUT ** -0.5),
        "router_group_w": nrm(ks[23], (DEPTH, D_MODEL, MOE_GROUPS), D_MODEL ** -0.5),
        "router_group_b": nrm(ks[24], (DEPTH, MOE_GROUPS), 0.01),
        "router_expert_w": nrm(ks[25], (DEPTH, D_MODEL, N_EXPERTS), D_MODEL ** -0.5),
        "router_expert_b": nrm(ks[26], (DEPTH, N_EXPERTS), 0.01),
        "moe_w_gate": nrm(ks[27], (DEPTH, MOE_GROUPS, MOE_EXPERTS_PER_GROUP, D_MODEL, D_EXPERT), D_MODEL ** -0.5),
        "moe_w_up": nrm(ks[28], (DEPTH, MOE_GROUPS, MOE_EXPERTS_PER_GROUP, D_MODEL, D_EXPERT), D_MODEL ** -0.5),
        "moe_w_down": nrm(ks[29], (DEPTH, MOE_GROUPS, MOE_EXPERTS_PER_GROUP, D_EXPERT, D_MODEL), D_EXPERT ** -0.5),
    }


def reference(x, norm_mix, norm_ffn, even_w_in, mlstm_i_bias, mlstm_f_bias, mlstm_norm,
              gdn_conv, gdn_A_log, gdn_dt_bias, gdn_norm, even_w_out,
              odd_w_in, att_q_norm, att_k_norm, att_rel_bias, ssd_conv, ssd_conv_bias,
              ssd_A_log, ssd_dt_bias, ssd_D, ssd_norm, odd_w_out,
              router_group_w, router_group_b, router_expert_w, router_expert_b,
              moe_w_gate, moe_w_up, moe_w_down):
    for layer in range(DEPTH):
        h = rms_norm(x, norm_mix[layer])
        i = layer // 2
        if layer % 2 == 0:
            mix = even_mixer(h, even_w_in[i], mlstm_i_bias[i], mlstm_f_bias[i], mlstm_norm[i],
                             gdn_conv[i], gdn_A_log[i], gdn_dt_bias[i], gdn_norm[i], even_w_out[i])
        else:
            mix = odd_mixer(h, odd_w_in[i], att_q_norm[i], att_k_norm[i], att_rel_bias[i],
                            ssd_conv[i], ssd_conv_bias[i], ssd_A_log[i], ssd_dt_bias[i], ssd_D[i],
                            ssd_norm[i], odd_w_out[i])
        x = x + mix.astype(x.dtype)
        h = rms_norm(x, norm_ffn[layer])
        x = x + hierarchical_moe(h, router_group_w[layer], router_group_b[layer],
                                 router_expert_w[layer], router_expert_b[layer],
                                 moe_w_gate[layer], moe_w_up[layer], moe_w_down[layer]).astype(x.dtype)
    return x
```

```python
import functools

import jax
import jax.numpy as jnp
from jax import lax
from jax.experimental import pallas as pl
from jax.experimental.pallas import tpu as pltpu

F32 = jnp.float32
BF16 = jnp.bfloat16
I32 = jnp.int32
HIGHEST = lax.Precision.HIGHEST

D_MODEL = 1024
CHUNK = 64
CONV_WIDTH = 4
EPS = 1e-6
LANES = 128

MLSTM_HEADS, MLSTM_DQK, MLSTM_DV = 4, 128, 256
GDN_HEADS, GDN_DK, GDN_DV = 8, 128, 128
ATT_HEADS, ATT_DH, ATT_LEFT_CHUNKS, REL_MAX_PAST = 16, 64, 8, 256
SSD_HEADS, SSD_DH, SSD_STATE, SSD_GROUPS = 32, 64, 128, 4
SSD_HPG = SSD_HEADS // SSD_GROUPS
SSD_INNER = SSD_HEADS * SSD_DH
SSD_BC_W = SSD_GROUPS * SSD_STATE
MOE_GROUPS, MOE_EPG, D_EXPERT = 4, 8, 256

VMEM_LIMIT = 48 * 1024 * 1024


def _cparams(*sem):
    return pltpu.CompilerParams(dimension_semantics=sem, vmem_limit_bytes=VMEM_LIMIT)


def _softplus(x):
    return jnp.maximum(x, 0.0) + jnp.log1p(jnp.exp(-jnp.abs(x)))


def _sigmoid(x):
    return 1.0 / (1.0 + jnp.exp(-x))


def _silu(x):
    return x * _sigmoid(x)


def _rms_rows(x):
    return x * lax.rsqrt(jnp.mean(x * x, axis=-1, keepdims=True) + EPS)


def _dot(a, b):
    return jnp.dot(a, b, preferred_element_type=F32)


def _dot_nt(a, b):
    return lax.dot_general(a, b, (((1,), (1,)), ((), ())), preferred_element_type=F32)


def _dot_tn(a, b):
    return lax.dot_general(a, b, (((0,), (0,)), ((), ())), preferred_element_type=F32)


def _tri(n, strict=False):
    r = lax.broadcasted_iota(I32, (n, n), 0)
    c = lax.broadcasted_iota(I32, (n, n), 1)
    return (r > c) if strict else (r >= c)


def _norm_matmul_body(x_ref, g_ref, w_ref, o_ref, h_scr):
    @pl.when(pl.program_id(1) == 0)
    def _():
        h_scr[...] = (_rms_rows(x_ref[...]) * g_ref[...]).astype(BF16)

    o_ref[...] = _dot(h_scr[...], w_ref[...]).astype(o_ref.dtype)


def _norm_matmul(x, gain, w, tm=1024, tn=1024):
    t, d = x.shape
    n = w.shape[1]
    tm = min(tm, t)
    return pl.pallas_call(
        _norm_matmul_body,
        grid=(t // tm, n // tn),
        in_specs=[
            pl.BlockSpec((tm, d), lambda i, j: (i, 0)),
            pl.BlockSpec((1, d), lambda i, j: (0, 0)),
            pl.BlockSpec((d, tn), lambda i, j: (0, j)),
        ],
        out_specs=pl.BlockSpec((tm, tn), lambda i, j: (i, j)),
        out_shape=jax.ShapeDtypeStruct((t, n), BF16),
        scratch_shapes=[pltpu.VMEM((tm, d), BF16)],
        compiler_params=_cparams("parallel", "arbitrary"),
        name="norm_matmul",
    )(x, gain.reshape(1, d), w)


def _gates_body(x_ref, g_ref, w_ref, aux_ref, col_ref, row_ref, *, lanes):
    h = _rms_rows(x_ref[...]) * g_ref[...]
    pre = jnp.dot(h, w_ref[...], precision=HIGHEST, preferred_element_type=F32) + aux_ref[0:1, :]
    scale = aux_ref[1:2, :]
    lane = lax.broadcasted_iota(I32, pre.shape, 1)
    sp = _softplus(pre)
    vals = jnp.zeros_like(pre)
    cum_mask = jnp.zeros(pre.shape, jnp.bool_)
    for lo, hi, kind, cum in lanes:
        m = (lane >= lo) & (lane < hi)
        if kind == "id":
            v = pre
        elif kind == "logsig":
            v = pre - sp
        elif kind == "softplus":
            v = scale * sp
        else:
            v = _sigmoid(pre)
        vals = jnp.where(m, v, vals)
        if cum:
            cum_mask = cum_mask | m
    pos = lax.broadcasted_iota(I32, pre.shape, 0) & (CHUNK - 1)
    c = vals
    k = 1
    while k < CHUNK:
        c = c + jnp.where(pos >= k, pltpu.roll(c, k, 0), 0.0)
        k *= 2
    out = jnp.where(cum_mask, c, vals)
    col_ref[...] = out
    row_ref[...] = out.T


def _gates(x, gain, w, aux, lanes, tm=512):
    t, d = x.shape
    tm = min(tm, t)
    return pl.pallas_call(
        functools.partial(_gates_body, lanes=lanes),
        grid=(t // tm,),
        in_specs=[
            pl.BlockSpec((tm, d), lambda i: (i, 0)),
            pl.BlockSpec((1, d), lambda i: (0, 0)),
            pl.BlockSpec((d, LANES), lambda i: (0, 0)),
            pl.BlockSpec((8, LANES), lambda i: (0, 0)),
        ],
        out_specs=[
            pl.BlockSpec((tm, LANES), lambda i: (i, 0)),
            pl.BlockSpec((LANES, tm), lambda i: (0, i)),
        ],
        out_shape=[jax.ShapeDtypeStruct((t, LANES), F32), jax.ShapeDtypeStruct((LANES, t), F32)],
        compiler_params=_cparams("parallel"),
        name="gates",
    )(x, gain.reshape(1, d), w, aux)


def _mlstm_body(q_ref, k_ref, v_ref, og_ref, gcol_ref, grow_ref, mn_ref, o_ref, c_scr, n_scr, m_scr, *, ts):
    @pl.when(pl.program_id(1) == 0)
    def _():
        c_scr[...] = jnp.zeros_like(c_scr)
        n_scr[...] = jnp.zeros_like(n_scr)
        m_scr[...] = jnp.zeros_like(m_scr)

    scale = MLSTM_DQK ** -0.5
    tri = _tri(CHUNK)
    for c in range(ts // CHUNK):
        rows = slice(c * CHUNK, (c + 1) * CHUNK)
        gcol = gcol_ref[rows, :]
        for h in range(MLSTM_HEADS):
            q = q_ref[rows, h * MLSTM_DQK:(h + 1) * MLSTM_DQK]
            k = k_ref[rows, h * MLSTM_DQK:(h + 1) * MLSTM_DQK]
            v = v_ref[rows, h * MLSTM_DV:(h + 1) * MLSTM_DV]
            li_col = gcol[:, h:h + 1]
            b_col = gcol[:, 4 + h:5 + h]
            li_row = grow_ref[h:h + 1, rows]
            b_row = grow_ref[4 + h:5 + h, rows]
            c_prev = c_scr[h]
            n_prev = n_scr[h][0:1, :]
            m_prev = m_scr[h][0:1, 0:1]

            b_end = b_col[CHUNK - 1:CHUNK, :]
            g_end = b_end - b_col + li_col
            m_loc = jnp.max(g_end, axis=0, keepdims=True)
            kw = k.astype(F32) * jnp.exp(g_end - m_loc)
            upd_c = _dot_tn(kw.astype(BF16), v)
            upd_n = jnp.sum(kw, axis=0, keepdims=True)

            log_d = jnp.where(tri, b_col - b_row + li_row, -jnp.inf)
            inter_log = b_col + m_prev
            m_t = jnp.maximum(inter_log, jnp.max(log_d, axis=1, keepdims=True))
            w_intra = jnp.exp(log_d - m_t)
            w_inter = jnp.exp(inter_log - m_t)
            scores = _dot_nt(q, k) * (scale * w_intra)
            qc = _dot(q, c_prev.astype(BF16)) * scale
            num = _dot(scores.astype(BF16), v) + w_inter * qc
            qn = jnp.sum(q.astype(F32) * n_prev, axis=1, keepdims=True) * scale
            den = jnp.sum(scores, axis=1, keepdims=True) + w_inter * qn
            hout = num / jnp.maximum(jnp.abs(den), jnp.exp(-m_t))

            m_new = jnp.maximum(b_end + m_prev, m_loc)
            s_old = jnp.exp(b_end + m_prev - m_new)
            s_upd = jnp.exp(m_loc - m_new)
            c_scr[h] = s_old * c_prev + s_upd * upd_c
            n_scr[h] = jnp.broadcast_to(s_old * n_prev + s_upd * upd_n, n_scr.shape[1:])
            m_scr[h] = jnp.broadcast_to(m_new, m_scr.shape[1:])

            cols = slice(h * MLSTM_DV, (h + 1) * MLSTM_DV)
            gate = _sigmoid(og_ref[rows, cols].astype(F32))
            o_ref[rows, cols] = (_rms_rows(hout) * mn_ref[:, cols] * gate).astype(o_ref.dtype)


def _mlstm(p, gcol, grow, m_norm, batch, seq, ts=256):
    t = batch * seq
    ts = min(ts, seq)
    nt = seq // ts
    qk_w = MLSTM_HEADS * MLSTM_DQK
    v_w = MLSTM_HEADS * MLSTM_DV
    row = lambda b, i: b * nt + i
    return pl.pallas_call(
        functools.partial(_mlstm_body, ts=ts),
        grid=(batch, nt),
        in_specs=[
            pl.BlockSpec((ts, qk_w), lambda b, i: (row(b, i), 0)),
            pl.BlockSpec((ts, qk_w), lambda b, i: (row(b, i), 1)),
            pl.BlockSpec((ts, v_w), lambda b, i: (row(b, i), 1)),
            pl.BlockSpec((ts, v_w), lambda b, i: (row(b, i), 2)),
            pl.BlockSpec((ts, LANES), lambda b, i: (row(b, i), 0)),
            pl.BlockSpec((8, ts), lambda b, i: (0, row(b, i))),
            pl.BlockSpec((1, v_w), lambda b, i: (0, 0)),
        ],
        out_specs=pl.BlockSpec((ts, v_w), lambda b, i: (row(b, i), 0)),
        out_shape=jax.ShapeDtypeStruct((t, v_w), BF16),
        scratch_shapes=[
            pltpu.VMEM((MLSTM_HEADS, MLSTM_DQK, MLSTM_DV), F32),
            pltpu.VMEM((MLSTM_HEADS, 8, MLSTM_DQK), F32),
            pltpu.VMEM((MLSTM_HEADS, 8, LANES), F32),
        ],
        compiler_params=_cparams("parallel", "arbitrary"),
        name="mlstm",
    )(p, p, p, p, gcol, grow, m_norm.reshape(1, v_w))


def _causal_conv(x_ref, tail_ref, w_ref, cols, ts, bias=None):
    xp = jnp.concatenate([tail_ref[:, cols], x_ref[:, cols].astype(F32)], axis=0)
    y = w_ref[CONV_WIDTH - 1:CONV_WIDTH, cols] * xp[8:, :]
    for j in range(CONV_WIDTH - 1):
        y = y + w_ref[j:j + 1, cols] * pltpu.roll(xp, CONV_WIDTH - 1 - j, 0)[8:, :]
    if bias is not None:
        y = y + bias[:, cols]
    return y


def _gdn_body(x_ref, z_ref, gcol_ref, grow_ref, cw_ref, gn_ref, o_ref, tail_scr, s_scr, *, ts):
    @pl.when(pl.program_id(1) == 0)
    def _():
        tail_scr[...] = jnp.zeros_like(tail_scr)
        s_scr[...] = jnp.zeros_like(s_scr)

    qk_w = GDN_HEADS * GDN_DK
    tri = _tri(CHUNK)
    tri_strict = _tri(CHUNK, strict=True)
    eye = (lax.broadcasted_iota(I32, (CHUNK, CHUNK), 0) == lax.broadcasted_iota(I32, (CHUNK, CHUNK), 1)).astype(F32)
    for h in range(GDN_HEADS):
        qcols = slice(h * GDN_DK, (h + 1) * GDN_DK)
        kcols = slice(qk_w + h * GDN_DK, qk_w + (h + 1) * GDN_DK)
        vcols = slice(2 * qk_w + h * GDN_DV, 2 * qk_w + (h + 1) * GDN_DV)
        q_all = _silu(_causal_conv(x_ref, tail_scr, cw_ref, qcols, ts))
        k_all = _silu(_causal_conv(x_ref, tail_scr, cw_ref, kcols, ts))
        v_all = _silu(_causal_conv(x_ref, tail_scr, cw_ref, vcols, ts))
        q_all = q_all * lax.rsqrt(jnp.sum(q_all * q_all, axis=-1, keepdims=True) + EPS) * (GDN_DK ** -0.5)
        k_all = k_all * lax.rsqrt(jnp.sum(k_all * k_all, axis=-1, keepdims=True) + EPS)
        for c in range(ts // CHUNK):
            rows = slice(c * CHUNK, (c + 1) * CHUNK)
            q = q_all[rows]
            k = k_all[rows]
            v = v_all[rows]
            g_col = gcol_ref[rows, 16 + h:17 + h]
            beta = gcol_ref[rows, 24 + h:25 + h]
            g_row = grow_ref[h:h + 1, rows]
            g_end = g_col[CHUNK - 1:CHUNK, :]
            qb = q.astype(BF16)
            kb = k.astype(BF16)
            decay = jnp.exp(jnp.where(tri, g_col - g_row, -jnp.inf))
            a_mat = jnp.where(tri_strict, beta * _dot_nt(kb, kb) * decay, 0.0)
            p = -a_mat
            y = eye + p
            for level in range(5):
                pb = p.astype(BF16)
                if level == 0:
                    p = _dot(pb, pb)
                else:
                    py = _dot(pb, jnp.concatenate([y, p], axis=1).astype(BF16))
                    y = y + py[:, :CHUNK]
                    p = py[:, CHUNK:]
            y = y + _dot(p.astype(BF16), y.astype(BF16))
            eg = jnp.exp(g_col)
            rhs = jnp.concatenate([beta * v, (beta * eg) * k], axis=1).astype(BF16)
            sol = _dot(y.astype(BF16), rhs)
            u = sol[:, :GDN_DV]
            w = sol[:, GDN_DV:]
            qk = _dot_nt(qb, kb) * decay
            q_dec = q * eg
            k_dec = k * jnp.exp(g_end - g_col)
            s_prev = s_scr[h]
            ws = _dot(jnp.concatenate([w, q_dec], axis=0).astype(BF16), s_prev.astype(BF16))
            v_new = u - ws[:CHUNK]
            vb = v_new.astype(BF16)
            o = ws[CHUNK:] + _dot(qk.astype(BF16), vb)
            s_scr[h] = jnp.exp(g_end) * s_prev + _dot_tn(k_dec.astype(BF16), vb)
            gate = _silu(z_ref[rows, h * GDN_DV:(h + 1) * GDN_DV].astype(F32))
            o_ref[rows, h * GDN_DV:(h + 1) * GDN_DV] = (_rms_rows(o) * gn_ref[...] * gate).astype(o_ref.dtype)
    tail_scr[...] = x_ref[ts - 8:ts, :].astype(F32)


def _gdn(p, gcol, grow, conv_w, g_norm, batch, seq, ts=256):
    t = batch * seq
    ts = min(ts, seq)
    nt = seq // ts
    qkv_w = 3 * GDN_HEADS * GDN_DK
    v_w = GDN_HEADS * GDN_DV
    row = lambda b, i: b * nt + i
    return pl.pallas_call(
        functools.partial(_gdn_body, ts=ts),
        grid=(batch, nt),
        in_specs=[
            pl.BlockSpec((ts, qkv_w), lambda b, i: (row(b, i), 1)),
            pl.BlockSpec((ts, v_w), lambda b, i: (row(b, i), 6)),
            pl.BlockSpec((ts, LANES), lambda b, i: (row(b, i), 0)),
            pl.BlockSpec((8, ts), lambda b, i: (2, row(b, i))),
            pl.BlockSpec((CONV_WIDTH, qkv_w), lambda b, i: (0, 0)),
            pl.BlockSpec((1, GDN_DV), lambda b, i: (0, 0)),
        ],
        out_specs=pl.BlockSpec((ts, v_w), lambda b, i: (row(b, i), 0)),
        out_shape=jax.ShapeDtypeStruct((t, v_w), BF16),
        scratch_shapes=[
            pltpu.VMEM((8, qkv_w), F32),
            pltpu.VMEM((GDN_HEADS, GDN_DK, GDN_DV), F32),
        ],
        compiler_params=_cparams("parallel", "arbitrary"),
        name="gdn",
    )(p, p, gcol, grow, conv_w, g_norm.reshape(1, GDN_DV))


def _out_proj_body(x_ref, a_ref, b_ref, wa_ref, wb_ref, o_ref):
    o_ref[...] = x_ref[...] + _dot(a_ref[...], wa_ref[...]) + _dot(b_ref[...], wb_ref[...])


def _out_proj(x, a, b, wa, wb, tm=512):
    t, d = x.shape
    tm = min(tm, t)
    ka, kb = a.shape[1], b.shape[1]
    return pl.pallas_call(
        _out_proj_body,
        grid=(t // tm,),
        in_specs=[
            pl.BlockSpec((tm, d), lambda i: (i, 0)),
            pl.BlockSpec((tm, ka), lambda i: (i, 0)),
            pl.BlockSpec((tm, kb), lambda i: (i, 0)),
            pl.BlockSpec((ka, d), lambda i: (0, 0)),
            pl.BlockSpec((kb, d), lambda i: (0, 0)),
        ],
        out_specs=pl.BlockSpec((tm, d), lambda i: (i, 0)),
        out_shape=jax.ShapeDtypeStruct((t, d), F32),
        compiler_params=_cparams("parallel"),
        name="out_proj",
    )(x, a, b, wa, wb)


def _lane_table(entries):
    row = jnp.zeros((LANES,), F32)
    for start, vals in entries:
        row = lax.dynamic_update_slice(row, vals.astype(F32).reshape(-1), (start,))
    return row.reshape(1, LANES)


def _gate_weight(w_in, entries):
    w = jnp.zeros((w_in.shape[0], LANES), F32)
    for start, lo, hi in entries:
        w = lax.dynamic_update_slice(w, w_in[:, lo:hi].astype(F32), (0, start))
    return w


EVEN_GATE_LANES = ((0, 4, "id", False), (4, 8, "logsig", True), (16, 24, "softplus", True), (24, 32, "sigmoid", False))


def _even_mix(xf, batch, seq, gain, w_in, i_bias, f_bias, m_norm, conv_w, a_log, dt_bias, g_norm, w_out):
    w_big = jnp.concatenate([w_in[:, 0:3072], w_in[:, 3080:7176]], axis=1).astype(BF16)
    w_gate = _gate_weight(w_in, ((0, 3072, 3076), (4, 3076, 3080), (16, 7176, 7184), (24, 7184, 7192)))
    aux = jnp.concatenate([
        _lane_table(((0, i_bias), (4, f_bias), (16, dt_bias))),
        _lane_table(((16, -jnp.exp(a_log)),)),
        jnp.zeros((6, LANES), F32)], axis=0)
    p = _norm_matmul(xf, gain, w_big)
    gcol, grow = _gates(xf, gain, w_gate, aux, EVEN_GATE_LANES)
    hm = _mlstm(p, gcol, grow, m_norm, batch, seq)
    o = _gdn(p, gcol, grow, conv_w, g_norm, batch, seq)
    v_w = MLSTM_HEADS * MLSTM_DV
    return _out_proj(xf, hm, o, w_out[:v_w].astype(BF16), w_out[v_w:].astype(BF16))


ATT_TQ = 256


def _attn_body(q_ref, k0_ref, k1_ref, k2_ref, v0_ref, v1_ref, v2_ref, bias_ref, qg_ref, kg_ref, o_ref):
    i = pl.program_id(2)

    def head_norm(x):
        r = x.shape[0]
        x3 = x.reshape(r, 2, ATT_DH)
        return (x3 * lax.rsqrt(jnp.mean(x3 * x3, axis=-1, keepdims=True) + EPS)).reshape(r, 2 * ATT_DH)

    qn = (head_norm(q_ref[...].astype(F32)) * qg_ref[...]).astype(BF16)
    kcat = jnp.concatenate([k0_ref[...], k1_ref[...], k2_ref[...]], axis=0).astype(F32)
    kn = (head_norm(kcat) * kg_ref[...]).astype(BF16)
    vcat = jnp.concatenate([v0_ref[...], v1_ref[...], v2_ref[...]], axis=0)
    tq = q_ref.shape[0]
    valid = lax.broadcasted_iota(I32, (tq, 3 * tq), 1) >= (2 - i) * tq
    outs = []
    for hh in range(2):
        cs = slice(hh * ATT_DH, (hh + 1) * ATT_DH)
        s = jnp.where(valid, _dot_nt(qn[:, cs], kn[:, cs]) + bias_ref[hh], -jnp.inf)
        p = jnp.exp(s - jnp.max(s, axis=-1, keepdims=True))
        l = jnp.sum(p, axis=-1, keepdims=True)
        outs.append(_dot(p.astype(BF16), vcat[:, cs]) / l)
    o_ref[...] = jnp.concatenate(outs, axis=1).astype(o_ref.dtype)


def _attn(p, bias, qg, kg, batch, seq, col0):
    t = batch * seq
    tq = ATT_TQ
    nt = seq // tq
    npair = ATT_HEADS // 2
    row = lambda b, i, back: b * nt + jnp.maximum(i - back, 0)
    kv_spec = lambda off, back: pl.BlockSpec((tq, LANES), lambda hp, b, i: (row(b, i, back), col0 + off + hp))
    return pl.pallas_call(
        _attn_body,
        grid=(npair, batch, nt),
        in_specs=[
            pl.BlockSpec((tq, LANES), lambda hp, b, i: (row(b, i, 0), col0 + hp)),
            kv_spec(npair, 2), kv_spec(npair, 1), kv_spec(npair, 0),
            kv_spec(2 * npair, 2), kv_spec(2 * npair, 1), kv_spec(2 * npair, 0),
            pl.BlockSpec((2, tq, 3 * tq), lambda hp, b, i: (hp, 0, 0)),
            pl.BlockSpec((1, LANES), lambda hp, b, i: (0, 0)),
            pl.BlockSpec((1, LANES), lambda hp, b, i: (0, 0)),
        ],
        out_specs=pl.BlockSpec((tq, LANES), lambda hp, b, i: (row(b, i, 0), hp)),
        out_shape=jax.ShapeDtypeStruct((t, ATT_HEADS * ATT_DH), BF16),
        compiler_params=_cparams("parallel", "parallel", "arbitrary"),
        name="band_attn",
    )(p, p, p, p, p, p, p, bias, qg, kg)


def _attn_bias_table(rel_bias):
    pad = ATT_LEFT_CHUNKS * CHUNK
    band = pad + CHUNK
    rel = jnp.arange(CHUNK)[:, None] + pad - jnp.arange(band)[None, :]
    rel_idx = jnp.clip(rel, -(CHUNK - 1), REL_MAX_PAST) + (CHUNK - 1)
    bias64 = rel_bias.astype(F32)[:, rel_idx]
    nq = ATT_TQ // CHUNK
    rows = [jnp.pad(bias64, ((0, 0), (0, 0), (a * CHUNK, (nq - 1 - a) * CHUNK)), constant_values=-jnp.inf)
            for a in range(nq)]
    return jnp.concatenate(rows, axis=1)


def _ssd_body(x_ref, z_ref, gcol_ref, grow_ref, cw_ref, cb_ref, dsk_ref, sn_ref, o_ref,
              tail_scr, xc_scr, y_scr, s_scr, *, ts):
    @pl.when(pl.program_id(1) == 0)
    def _():
        tail_scr[...] = jnp.zeros_like(tail_scr)
        s_scr[...] = jnp.zeros_like(s_scr)

    conv_w = x_ref.shape[1]
    for j in range(conv_w // LANES):
        cols = slice(j * LANES, (j + 1) * LANES)
        xc_scr[:, cols] = _silu(_causal_conv(x_ref, tail_scr, cw_ref, cols, ts, bias=cb_ref))

    tri = _tri(CHUNK)
    gw = SSD_HPG * SSD_DH
    for c in range(ts // CHUNK):
        rows = slice(c * CHUNK, (c + 1) * CHUNK)
        for g in range(SSD_GROUPS):
            bc = xc_scr[rows, SSD_INNER + g * SSD_STATE:SSD_INNER + (g + 1) * SSD_STATE].astype(BF16)
            cc = xc_scr[rows, SSD_INNER + SSD_BC_W + g * SSD_STATE:SSD_INNER + SSD_BC_W + (g + 1) * SSD_STATE].astype(BF16)
            cb = _dot_nt(cc, bc)
            s_prev = s_scr[g]
            y_off = _dot(cc, s_prev.astype(BF16))
            dxd_parts, cdec_parts = [], []
            for j in range(SSD_HPG):
                hd = g * SSD_HPG + j
                hc = slice(hd * SSD_DH, (hd + 1) * SSD_DH)
                dt_col = gcol_ref[rows, hd:hd + 1]
                a_col = gcol_ref[rows, SSD_HEADS + hd:SSD_HEADS + hd + 1]
                a_row = grow_ref[hd:hd + 1, rows]
                a_end = a_col[CHUNK - 1:CHUNK, :]
                xj = xc_scr[rows, hc]
                xd = xj * dt_col
                decay = jnp.exp(jnp.where(tri, a_col - a_row, -jnp.inf))
                y = (_dot((cb * decay).astype(BF16), xd.astype(BF16))
                     + y_off[:, j * SSD_DH:(j + 1) * SSD_DH] * jnp.exp(a_col) + dsk_ref[:, hc] * xj)
                y_scr[rows, hc] = y
                dxd_parts.append(xd * jnp.exp(a_end - a_col))
                cdec_parts.append(jnp.broadcast_to(jnp.exp(a_end), (1, SSD_DH)))
            dxd = jnp.concatenate(dxd_parts, axis=1).astype(BF16)
            cdec = jnp.concatenate(cdec_parts, axis=1)
            s_scr[g] = s_prev * cdec + _dot_tn(bc, dxd)

    for g in range(SSD_GROUPS):
        cols = slice(g * gw, (g + 1) * gw)
        yg = y_scr[:, cols] * _silu(z_ref[:, cols].astype(F32))
        o_ref[:, cols] = (_rms_rows(yg) * sn_ref[:, cols]).astype(o_ref.dtype)
    tail_scr[...] = x_ref[ts - 8:ts, :].astype(F32)


def _ssd(p, gcol, grow, conv_w, conv_b, d_skip, s_norm, batch, seq, ts=256):
    t = batch * seq
    ts = min(ts, seq)
    nt = seq // ts
    cw = SSD_INNER + 2 * SSD_BC_W
    row = lambda b, i: b * nt + i
    return pl.pallas_call(
        functools.partial(_ssd_body, ts=ts),
        grid=(batch, nt),
        in_specs=[
            pl.BlockSpec((ts, cw), lambda b, i: (row(b, i), 0)),
            pl.BlockSpec((ts, SSD_INNER), lambda b, i: (row(b, i), 3)),
            pl.BlockSpec((ts, LANES), lambda b, i: (row(b, i), 0)),
            pl.BlockSpec((SSD_HEADS, ts), lambda b, i: (1, row(b, i))),
            pl.BlockSpec((CONV_WIDTH, cw), lambda b, i: (0, 0)),
            pl.BlockSpec((1, cw), lambda b, i: (0, 0)),
            pl.BlockSpec((1, SSD_INNER), lambda b, i: (0, 0)),
            pl.BlockSpec((1, SSD_INNER), lambda b, i: (0, 0)),
        ],
        out_specs=pl.BlockSpec((ts, SSD_INNER), lambda b, i: (row(b, i), 0)),
        out_shape=jax.ShapeDtypeStruct((t, SSD_INNER), BF16),
        scratch_shapes=[
            pltpu.VMEM((8, cw), F32),
            pltpu.VMEM((ts, cw), F32),
            pltpu.VMEM((ts, SSD_INNER), F32),
            pltpu.VMEM((SSD_GROUPS, SSD_STATE, SSD_HPG * SSD_DH), F32),
        ],
        compiler_params=_cparams("parallel", "arbitrary"),
        name="ssd",
    )(p, p, gcol, grow, conv_w, conv_b.reshape(1, cw), jnp.repeat(d_skip, SSD_DH).reshape(1, SSD_INNER),
      s_norm.reshape(1, SSD_INNER))


ODD_GATE_LANES = ((0, 32, "softplus", False), (32, 64, "softplus", True))


def _odd_mix(xf, batch, seq, gain, w_in, q_gain, k_gain, rel_bias, conv_w, conv_b, a_log, dt_bias, d_skip, s_norm, w_out):
    w_big = jnp.concatenate([w_in[:, 5120:8192], w_in[:, 0:3072], w_in[:, 3072:5120]], axis=1).astype(BF16)
    w_gate = _gate_weight(w_in, ((0, 8192, 8224), (SSD_HEADS, 8192, 8224)))
    aux = jnp.concatenate([
        _lane_table(((0, dt_bias), (SSD_HEADS, dt_bias))),
        _lane_table(((0, jnp.ones((SSD_HEADS,), F32)), (SSD_HEADS, -jnp.exp(a_log)))),
        jnp.zeros((6, LANES), F32)], axis=0)
    p = _norm_matmul(xf, gain, w_big)
    gcol, grow = _gates(xf, gain, w_gate, aux, ODD_GATE_LANES)
    qg = jnp.tile(q_gain.astype(F32) * ATT_DH ** -0.5, 2).reshape(1, LANES)
    kg = jnp.tile(k_gain.astype(F32), 2).reshape(1, LANES)
    attn = _attn(p, _attn_bias_table(rel_bias), qg, kg, batch, seq, col0=3072 // LANES)
    y = _ssd(p, gcol, grow, conv_w, conv_b, d_skip, s_norm, batch, seq)
    a_w = ATT_HEADS * ATT_DH
    return _out_proj(xf, attn, y, w_out[:a_w].astype(BF16), w_out[a_w:].astype(BF16))


ROW_W = D_MODEL + LANES
MOE_TM = 1024
DMA_ROWS = 256


def _router_body(x_ref, g_ref, w_ref, b_ref, rows_ref, info_ref, cnt_ref, carry_scr):
    @pl.when(pl.program_id(0) == 0)
    def _():
        carry_scr[...] = jnp.zeros_like(carry_scr)

    h = _rms_rows(x_ref[...]) * g_ref[...]
    logits = jnp.dot(h, w_ref[...], precision=HIGHEST, preferred_element_type=F32) + b_ref[...]
    tm = logits.shape[0]
    lane = lax.broadcasted_iota(I32, logits.shape, 1).astype(F32)
    ninf = -jnp.inf

    def first_max(v):
        m = jnp.max(v, axis=-1, keepdims=True)
        return m, jnp.min(jnp.where(v == m, lane, float(LANES)), axis=-1, keepdims=True)

    glog = jnp.where(lane < MOE_GROUPS, logits, ninf)
    gmax, gidx = first_max(glog)
    grp_w = 1.0 / jnp.sum(jnp.exp(glog - gmax), axis=-1, keepdims=True)
    e_lo = MOE_GROUPS + gidx * MOE_EPG
    in_grp = (lane >= e_lo) & (lane < e_lo + MOE_EPG)
    elog = jnp.where(in_grp, logits, ninf)
    v1, i1 = first_max(elog)
    v2, i2 = first_max(jnp.where(lane == i1, ninf, elog))
    e21 = jnp.exp(v2 - v1)
    w1 = 1.0 / (1.0 + e21)
    comb = grp_w * (jnp.where(lane == i1, w1, 0.0) + jnp.where(lane == i2, e21 * w1, 0.0))
    local = jnp.zeros_like(comb)
    for g in range(MOE_GROUPS):
        local = local + pltpu.roll(comb, LANES - MOE_GROUPS - g * MOE_EPG, 1)
    local = jnp.where(lane < MOE_EPG, local, 0.0)
    rows_ref[...] = jnp.concatenate([h, local], axis=1)

    onehot = jnp.where(lane == gidx, 1.0, 0.0)
    lower = jnp.where(_tri(tm, strict=True), 1.0, 0.0).astype(BF16)
    before = _dot(lower, onehot.astype(BF16)) + carry_scr[0:1, :]
    rank = jnp.sum(onehot * before, axis=-1, keepdims=True)
    carry = carry_scr[0:1, :] + jnp.sum(onehot, axis=0, keepdims=True)
    carry_scr[...] = jnp.broadcast_to(carry, carry_scr.shape)
    cnt_ref[...] = jnp.broadcast_to(carry, cnt_ref.shape)
    info = jnp.where(lane == 0, gidx, jnp.where(lane == 1, rank, 0.0))
    info_ref[...] = info.T[0:8, :]


def _router(x, gain, w, b, tm=256):
    t, d = x.shape
    tm = min(tm, t)
    return pl.pallas_call(
        _router_body,
        grid=(t // tm,),
        in_specs=[
            pl.BlockSpec((tm, d), lambda i: (i, 0)),
            pl.BlockSpec((1, d), lambda i: (0, 0)),
            pl.BlockSpec((d, LANES), lambda i: (0, 0)),
            pl.BlockSpec((1, LANES), lambda i: (0, 0)),
        ],
        out_specs=[
            pl.BlockSpec((tm, ROW_W), lambda i: (i, 0)),
            pl.BlockSpec((8, tm), lambda i: (0, i)),
            pl.BlockSpec((8, LANES), lambda i: (0, 0)),
        ],
        out_shape=[
            jax.ShapeDtypeStruct((t, ROW_W), F32),
            jax.ShapeDtypeStruct((8, t), F32),
            jax.ShapeDtypeStruct((8, LANES), F32),
        ],
        scratch_shapes=[pltpu.VMEM((8, LANES), F32)],
        compiler_params=_cparams("arbitrary"),
        name="moe_router",
    )(x, gain.reshape(1, d), w, b)


def _dispatch_body(pos_ref, rows_ref, init_ref, out_ref, sem):
    del init_ref
    base = pl.program_id(0) * DMA_ROWS

    def row_copy(r):
        return pltpu.make_async_copy(rows_ref.at[pl.ds(r, 1)], out_ref.at[pl.ds(pos_ref[base + r], 1)], sem)

    def start(r, carry):
        row_copy(r).start()
        return carry

    def wait(r, carry):
        row_copy(r).wait()
        return carry

    lax.fori_loop(0, DMA_ROWS, start, 0, unroll=8)
    lax.fori_loop(0, DMA_ROWS, wait, 0, unroll=8)


def _dispatch(pos, rows, n_sorted):
    t, w = rows.shape
    return pl.pallas_call(
        _dispatch_body,
        grid_spec=pltpu.PrefetchScalarGridSpec(
            num_scalar_prefetch=1,
            grid=(t // DMA_ROWS,),
            in_specs=[
                pl.BlockSpec((DMA_ROWS, w), lambda i, pos: (i, 0)),
                pl.BlockSpec(memory_space=pl.ANY),
            ],
            out_specs=pl.BlockSpec(memory_space=pl.ANY),
            scratch_shapes=[pltpu.SemaphoreType.DMA(())],
        ),
        out_shape=jax.ShapeDtypeStruct((n_sorted, w), rows.dtype),
        input_output_aliases={2: 0},
        compiler_params=_cparams("arbitrary"),
        name="moe_dispatch",
    )(pos, rows, jnp.zeros((n_sorted, w), rows.dtype))


def _experts_body(tile_group_ref, n_used_ref, xs_ref, wg_ref, wu_ref, wd_ref, y_ref, x_scr, acc_scr):
    del tile_group_ref
    i = pl.program_id(0)
    e = pl.program_id(1)

    @pl.when(i < n_used_ref[0])
    def _():
        @pl.when(e == 0)
        def _():
            x_scr[...] = xs_ref[:, :D_MODEL].astype(BF16)
            acc_scr[...] = jnp.zeros_like(acc_scr)

        x = x_scr[...]
        a = _dot(x, wg_ref[...].astype(BF16))
        u = _dot(x, wu_ref[...].astype(BF16))
        comb = xs_ref[:, D_MODEL:]
        lane = lax.broadcasted_iota(I32, comb.shape, 1)
        ce = jnp.sum(jnp.where(lane == e, comb, 0.0), axis=-1, keepdims=True)
        act = (_silu(a) * u * ce).astype(BF16)
        acc_scr[...] += _dot(act, wd_ref[...].astype(BF16))

        @pl.when(e == MOE_EPG - 1)
        def _():
            y_ref[...] = acc_scr[...]

    @pl.when((i >= n_used_ref[0]) & (e == 0))
    def _():
        y_ref[...] = jnp.zeros_like(y_ref)


def _experts(tile_group, n_used, xs, w_gate, w_up, w_down):
    n_sorted = xs.shape[0]
    tm = MOE_TM
    w_in_spec = pl.BlockSpec((None, None, D_MODEL, D_EXPERT), lambda i, e, tg, nu: (tg[i], e, 0, 0))
    return pl.pallas_call(
        _experts_body,
        grid_spec=pltpu.PrefetchScalarGridSpec(
            num_scalar_prefetch=2,
            grid=(n_sorted // tm, MOE_EPG),
            in_specs=[
                pl.BlockSpec((tm, ROW_W), lambda i, e, tg, nu: (i, 0)),
                w_in_spec,
                w_in_spec,
                pl.BlockSpec((None, None, D_EXPERT, D_MODEL), lambda i, e, tg, nu: (tg[i], e, 0, 0)),
            ],
            out_specs=pl.BlockSpec((tm, D_MODEL), lambda i, e, tg, nu: (i, 0)),
            scratch_shapes=[pltpu.VMEM((tm, D_MODEL), BF16), pltpu.VMEM((tm, D_MODEL), F32)],
        ),
        out_shape=jax.ShapeDtypeStruct((n_sorted, D_MODEL), F32),
        compiler_params=_cparams("parallel", "arbitrary"),
        name="moe_experts",
    )(tile_group, n_used, xs, w_gate, w_up, w_down)


def _combine_body(pos_ref, x_ref, y_ref, o_ref, buf, sem):
    base = pl.program_id(0) * DMA_ROWS

    def row_copy(r):
        return pltpu.make_async_copy(y_ref.at[pl.ds(pos_ref[base + r], 1)], buf.at[pl.ds(r, 1)], sem)

    def start(r, carry):
        row_copy(r).start()
        return carry

    def wait(r, carry):
        row_copy(r).wait()
        return carry

    lax.fori_loop(0, DMA_ROWS, start, 0, unroll=8)
    lax.fori_loop(0, DMA_ROWS, wait, 0, unroll=8)
    o_ref[...] = x_ref[...] + buf[...]


def _combine(pos, x, y):
    t, d = x.shape
    return pl.pallas_call(
        _combine_body,
        grid_spec=pltpu.PrefetchScalarGridSpec(
            num_scalar_prefetch=1,
            grid=(t // DMA_ROWS,),
            in_specs=[
                pl.BlockSpec((DMA_ROWS, d), lambda i, pos: (i, 0)),
                pl.BlockSpec(memory_space=pl.ANY),
            ],
            out_specs=pl.BlockSpec((DMA_ROWS, d), lambda i, pos: (i, 0)),
            scratch_shapes=[pltpu.VMEM((DMA_ROWS, d), F32), pltpu.SemaphoreType.DMA(())],
        ),
        out_shape=jax.ShapeDtypeStruct((t, d), F32),
        compiler_params=_cparams("arbitrary"),
        name="moe_combine",
    )(pos, x, y)


def _moe(xf, gain, wg, bg, we, be, w_gate, w_up, w_down):
    t = xf.shape[0]
    n_exp = MOE_GROUPS * MOE_EPG
    w_r = jnp.zeros((D_MODEL, LANES), F32)
    w_r = lax.dynamic_update_slice(w_r, wg.astype(F32), (0, 0))
    w_r = lax.dynamic_update_slice(w_r, we.astype(F32), (0, MOE_GROUPS))
    b_r = _lane_table(((0, bg), (MOE_GROUPS, be)))
    rows, info, cnt = _router(xf, gain, w_r, b_r)

    gidx = info[0].astype(I32)
    rank = info[1].astype(I32)
    counts = cnt[0, :MOE_GROUPS].astype(I32)
    padded = ((counts + MOE_TM - 1) // MOE_TM) * MOE_TM
    ends = jnp.cumsum(padded)
    starts = ends - padded
    pos = starts[gidx] + rank
    n_sorted = t + MOE_GROUPS * MOE_TM
    n_tiles = n_sorted // MOE_TM
    tile_start = jnp.arange(n_tiles, dtype=I32) * MOE_TM
    tile_group = jnp.minimum(jnp.sum(tile_start[:, None] >= ends[None, :], axis=1), MOE_GROUPS - 1).astype(I32)
    n_used = (ends[-1:] // MOE_TM).astype(I32)

    xs = _dispatch(pos, rows, n_sorted)
    ys = _experts(tile_group, n_used, xs, w_gate, w_up, w_down)
    return _combine(pos, xf, ys)


def kernel(x, norm_mix, norm_ffn, even_w_in, mlstm_i_bias, mlstm_f_bias, mlstm_norm, gdn_conv, gdn_A_log, gdn_dt_bias,
           gdn_norm, even_w_out, odd_w_in, att_q_norm, att_k_norm, att_rel_bias, ssd_conv, ssd_conv_bias, ssd_A_log,
           ssd_dt_bias, ssd_D, ssd_norm, odd_w_out, router_group_w, router_group_b, router_expert_w, router_expert_b,
           moe_w_gate, moe_w_up, moe_w_down):
    batch, seq, d = x.shape
    xf = x.reshape(batch * seq, d)
    depth = norm_mix.shape[0]
    for layer in range(depth):
        i = layer // 2
        if layer % 2 == 0:
            xf = _even_mix(xf, batch, seq, norm_mix[layer], even_w_in[i], mlstm_i_bias[i], mlstm_f_bias[i], mlstm_norm[i],
                           gdn_conv[i], gdn_A_log[i], gdn_dt_bias[i], gdn_norm[i], even_w_out[i])
        else:
            xf = _odd_mix(xf, batch, seq, norm_mix[layer], odd_w_in[i], att_q_norm[i], att_k_norm[i], att_rel_bias[i],
                          ssd_conv[i], ssd_conv_bias[i], ssd_A_log[i], ssd_dt_bias[i], ssd_D[i], ssd_norm[i], odd_w_out[i])
        xf = _moe(xf, norm_ffn[layer], router_group_w[layer], router_group_b[layer], router_expert_w[layer],
                  router_expert_b[layer], moe_w_gate[layer], moe_w_up[layer], moe_w_down[layer])
    return xf.reshape(batch, seq, d)
```

```python
import functools

import jax
import jax.numpy as jnp
from jax import lax
from jax.experimental import pallas as pl
from jax.experimental.pallas import tpu as pltpu

F32 = jnp.float32
BF16 = jnp.bfloat16
I32 = jnp.int32
HIGHEST = lax.Precision.HIGHEST

D_MODEL = 1024
CHUNK = 64
CONV_WIDTH = 4
EPS = 1e-6
LANES = 128

MLSTM_HEADS, MLSTM_DQK, MLSTM_DV = 4, 128, 256
GDN_HEADS, GDN_DK, GDN_DV = 8, 128, 128
ATT_HEADS, ATT_DH, ATT_LEFT_CHUNKS, REL_MAX_PAST = 16, 64, 8, 256
SSD_HEADS, SSD_DH, SSD_STATE, SSD_GROUPS = 32, 64, 128, 4
SSD_HPG = SSD_HEADS // SSD_GROUPS
SSD_INNER = SSD_HEADS * SSD_DH
SSD_BC_W = SSD_GROUPS * SSD_STATE
MOE_GROUPS, MOE_EPG, D_EXPERT = 4, 8, 256

VMEM_LIMIT = 48 * 1024 * 1024


def _cparams(*sem):
    return pltpu.CompilerParams(dimension_semantics=sem, vmem_limit_bytes=VMEM_LIMIT)


def _softplus(x):
    return jnp.maximum(x, 0.0) + jnp.log1p(jnp.exp(-jnp.abs(x)))


def _sigmoid(x):
    return 1.0 / (1.0 + jnp.exp(-x))


def _silu(x):
    return x * _sigmoid(x)


def _rms_rows(x):
    return x * lax.rsqrt(jnp.mean(x * x, axis=-1, keepdims=True) + EPS)


def _dot(a, b):
    return jnp.dot(a, b, preferred_element_type=F32)


def _dot_nt(a, b):
    return lax.dot_general(a, b, (((1,), (1,)), ((), ())), preferred_element_type=F32)


def _dot_tn(a, b):
    return lax.dot_general(a, b, (((0,), (0,)), ((), ())), preferred_element_type=F32)


def _bdot(a, b):
    return lax.dot_general(a, b, (((2,), (1,)), ((0,), (0,))), preferred_element_type=F32)


def _bdot_nt(a, b):
    return lax.dot_general(a, b, (((2,), (2,)), ((0,), (0,))), preferred_element_type=F32)


def _bdot_tn(a, b):
    return lax.dot_general(a, b, (((1,), (1,)), ((0,), (0,))), preferred_element_type=F32)


def _tri(n, strict=False):
    r = lax.broadcasted_iota(I32, (n, n), 0)
    c = lax.broadcasted_iota(I32, (n, n), 1)
    return (r > c) if strict else (r >= c)


def _norm_matmul_body(x_ref, g_ref, w_ref, o_ref, h_scr):
    @pl.when(pl.program_id(1) == 0)
    def _():
        h_scr[...] = (_rms_rows(x_ref[...]) * g_ref[...]).astype(BF16)

    o_ref[...] = _dot(h_scr[...], w_ref[...]).astype(o_ref.dtype)


def _norm_matmul(x, gain, w, tm=1024, tn=1024):
    t, d = x.shape
    n = w.shape[1]
    tm = min(tm, t)
    return pl.pallas_call(
        _norm_matmul_body,
        grid=(t // tm, n // tn),
        in_specs=[
            pl.BlockSpec((tm, d), lambda i, j: (i, 0)),
            pl.BlockSpec((1, d), lambda i, j: (0, 0)),
            pl.BlockSpec((d, tn), lambda i, j: (0, j)),
        ],
        out_specs=pl.BlockSpec((tm, tn), lambda i, j: (i, j)),
        out_shape=jax.ShapeDtypeStruct((t, n), BF16),
        scratch_shapes=[pltpu.VMEM((tm, d), BF16)],
        compiler_params=_cparams("parallel", "arbitrary"),
        name="norm_matmul",
    )(x, gain.reshape(1, d), w)


def _gates_body(x_ref, g_ref, w_ref, aux_ref, col_ref, row_ref, *, lanes):
    h = _rms_rows(x_ref[...]) * g_ref[...]
    pre = jnp.dot(h, w_ref[...], precision=HIGHEST, preferred_element_type=F32) + aux_ref[0:1, :]
    scale = aux_ref[1:2, :]
    lane = lax.broadcasted_iota(I32, pre.shape, 1)
    sp = _softplus(pre)
    vals = jnp.zeros_like(pre)
    cum_mask = jnp.zeros(pre.shape, jnp.bool_)
    for lo, hi, kind, cum in lanes:
        m = (lane >= lo) & (lane < hi)
        if kind == "id":
            v = pre
        elif kind == "logsig":
            v = pre - sp
        elif kind == "softplus":
            v = scale * sp
        else:
            v = _sigmoid(pre)
        vals = jnp.where(m, v, vals)
        if cum:
            cum_mask = cum_mask | m
    pos = lax.broadcasted_iota(I32, pre.shape, 0) & (CHUNK - 1)
    c = vals
    k = 1
    while k < CHUNK:
        c = c + jnp.where(pos >= k, pltpu.roll(c, k, 0), 0.0)
        k *= 2
    out = jnp.where(cum_mask, c, vals)
    col_ref[...] = out
    row_ref[...] = out.T


def _gates(x, gain, w, aux, lanes, tm=512):
    t, d = x.shape
    tm = min(tm, t)
    return pl.pallas_call(
        functools.partial(_gates_body, lanes=lanes),
        grid=(t // tm,),
        in_specs=[
            pl.BlockSpec((tm, d), lambda i: (i, 0)),
            pl.BlockSpec((1, d), lambda i: (0, 0)),
            pl.BlockSpec((d, LANES), lambda i: (0, 0)),
            pl.BlockSpec((8, LANES), lambda i: (0, 0)),
        ],
        out_specs=[
            pl.BlockSpec((tm, LANES), lambda i: (i, 0)),
            pl.BlockSpec((LANES, tm), lambda i: (0, i)),
        ],
        out_shape=[jax.ShapeDtypeStruct((t, LANES), F32), jax.ShapeDtypeStruct((LANES, t), F32)],
        compiler_params=_cparams("parallel"),
        name="gates",
    )(x, gain.reshape(1, d), w, aux)


def _mlstm_body(q_ref, k_ref, v_ref, og_ref, gcol_ref, grow_ref, mn_ref, o_ref, c_scr, n_scr, m_scr, *, ts):
    @pl.when(pl.program_id(1) == 0)
    def _():
        c_scr[...] = jnp.zeros_like(c_scr)
        n_scr[...] = jnp.zeros_like(n_scr)
        m_scr[...] = jnp.zeros_like(m_scr)

    scale = MLSTM_DQK ** -0.5
    nh, nc = MLSTM_HEADS, ts // CHUNK
    nb = nh * nc
    q = jnp.stack([q_ref[:, h * MLSTM_DQK:(h + 1) * MLSTM_DQK] for h in range(nh)], axis=0).reshape(nb, CHUNK, MLSTM_DQK)
    k = jnp.stack([k_ref[:, h * MLSTM_DQK:(h + 1) * MLSTM_DQK] for h in range(nh)], axis=0).reshape(nb, CHUNK, MLSTM_DQK)
    v = jnp.stack([v_ref[:, h * MLSTM_DV:(h + 1) * MLSTM_DV] for h in range(nh)], axis=0).reshape(nb, CHUNK, MLSTM_DV)
    gcol = gcol_ref[...]
    grow = grow_ref[...]
    li_col = jnp.stack([gcol[:, h:h + 1] for h in range(nh)], axis=0).reshape(nb, CHUNK, 1)
    b_col = jnp.stack([gcol[:, nh + h:nh + h + 1] for h in range(nh)], axis=0).reshape(nb, CHUNK, 1)
    li_row = jnp.stack([grow[0:nh, c * CHUNK:(c + 1) * CHUNK] for c in range(nc)], axis=1).reshape(nb, 1, CHUNK)
    b_row = jnp.stack([grow[nh:2 * nh, c * CHUNK:(c + 1) * CHUNK] for c in range(nc)], axis=1).reshape(nb, 1, CHUNK)

    b_end = b_col[:, CHUNK - 1:CHUNK, :]
    g_end = b_end - b_col + li_col
    m_loc = jnp.max(g_end, axis=1, keepdims=True)
    kw = k.astype(F32) * jnp.exp(g_end - m_loc)
    upd_c = _bdot_tn(kw.astype(BF16), v).reshape(nh, nc, MLSTM_DQK, MLSTM_DV)
    upd_n = jnp.sum(kw, axis=1, keepdims=True).reshape(nh, nc, 1, MLSTM_DQK)
    log_d = jnp.where(_tri(CHUNK)[None], b_col - b_row + li_row, -jnp.inf)
    max_d = jnp.max(log_d, axis=2, keepdims=True)
    s_raw = _bdot_nt(q, k)

    b_end4 = b_end.reshape(nh, nc, 1, 1)
    m_loc4 = m_loc.reshape(nh, nc, 1, 1)
    m_prev = m_scr[...][:, 0:1, 0:1]
    m_prevs, s_olds, s_upds = [], [], []
    for c in range(nc):
        m_prevs.append(m_prev)
        m_new = jnp.maximum(b_end4[:, c] + m_prev, m_loc4[:, c])
        s_olds.append(jnp.exp(b_end4[:, c] + m_prev - m_new))
        s_upds.append(jnp.exp(m_loc4[:, c] - m_new))
        m_prev = m_new
    m_scr[...] = jnp.broadcast_to(m_prev, m_scr.shape)

    inter_log = b_col + jnp.stack(m_prevs, axis=1).reshape(nb, 1, 1)
    m_t = jnp.maximum(inter_log, max_d)
    w_inter = jnp.exp(inter_log - m_t)
    scores = s_raw * (scale * jnp.exp(log_d - m_t))
    num = _bdot(scores.astype(BF16), v)
    den = jnp.sum(scores, axis=2, keepdims=True)

    q4 = q.reshape(nh, nc, CHUNK, MLSTM_DQK)
    c_state = c_scr[...]
    n_state = n_scr[...][:, 0:1, :]
    qcs, qns = [], []
    for c in range(nc):
        qcs.append(_bdot(q4[:, c], c_state.astype(BF16)))
        qns.append(jnp.sum(q4[:, c].astype(F32) * n_state, axis=2, keepdims=True))
        c_state = s_olds[c] * c_state + s_upds[c] * upd_c[:, c]
        n_state = s_olds[c] * n_state + s_upds[c] * upd_n[:, c]
    c_scr[...] = c_state
    n_scr[...] = jnp.broadcast_to(n_state, n_scr.shape)

    num = num + w_inter * (jnp.stack(qcs, axis=1).reshape(nb, CHUNK, MLSTM_DV) * scale)
    den = den + w_inter * (jnp.stack(qns, axis=1).reshape(nb, CHUNK, 1) * scale)
    hout = num / jnp.maximum(jnp.abs(den), jnp.exp(-m_t))
    hn = _rms_rows(hout).reshape(nh, ts, MLSTM_DV)
    for h in range(nh):
        cols = slice(h * MLSTM_DV, (h + 1) * MLSTM_DV)
        o_ref[:, cols] = (hn[h] * mn_ref[:, cols] * _sigmoid(og_ref[:, cols].astype(F32))).astype(o_ref.dtype)


def _mlstm(p, gcol, grow, m_norm, batch, seq, ts=256):
    t = batch * seq
    ts = min(ts, seq)
    nt = seq // ts
    qk_w = MLSTM_HEADS * MLSTM_DQK
    v_w = MLSTM_HEADS * MLSTM_DV
    row = lambda b, i: b * nt + i
    return pl.pallas_call(
        functools.partial(_mlstm_body, ts=ts),
        grid=(batch, nt),
        in_specs=[
            pl.BlockSpec((ts, qk_w), lambda b, i: (row(b, i), 0)),
            pl.BlockSpec((ts, qk_w), lambda b, i: (row(b, i), 1)),
            pl.BlockSpec((ts, v_w), lambda b, i: (row(b, i), 1)),
            pl.BlockSpec((ts, v_w), lambda b, i: (row(b, i), 2)),
            pl.BlockSpec((ts, LANES), lambda b, i: (row(b, i), 0)),
            pl.BlockSpec((8, ts), lambda b, i: (0, row(b, i))),
            pl.BlockSpec((1, v_w), lambda b, i: (0, 0)),
        ],
        out_specs=pl.BlockSpec((ts, v_w), lambda b, i: (row(b, i), 0)),
        out_shape=jax.ShapeDtypeStruct((t, v_w), BF16),
        scratch_shapes=[
            pltpu.VMEM((MLSTM_HEADS, MLSTM_DQK, MLSTM_DV), F32),
            pltpu.VMEM((MLSTM_HEADS, 8, MLSTM_DQK), F32),
            pltpu.VMEM((MLSTM_HEADS, 8, LANES), F32),
        ],
        compiler_params=_cparams("parallel", "arbitrary"),
        name="mlstm",
    )(p, p, p, p, gcol, grow, m_norm.reshape(1, v_w))


def _causal_conv(x_ref, tail_ref, w_ref, cols, ts, bias=None):
    xp = jnp.concatenate([tail_ref[:, cols], x_ref[:, cols].astype(F32)], axis=0)
    y = w_ref[CONV_WIDTH - 1:CONV_WIDTH, cols] * xp[8:, :]
    for j in range(CONV_WIDTH - 1):
        y = y + w_ref[j:j + 1, cols] * pltpu.roll(xp, CONV_WIDTH - 1 - j, 0)[8:, :]
    if bias is not None:
        y = y + bias[:, cols]
    return y


def _gdn_body(x_ref, z_ref, gcol_ref, grow_ref, cw_ref, gn_ref, o_ref, tail_scr, s_scr, *, ts):
    @pl.when(pl.program_id(1) == 0)
    def _():
        tail_scr[...] = jnp.zeros_like(tail_scr)
        s_scr[...] = jnp.zeros_like(s_scr)

    nh, nc = GDN_HEADS, ts // CHUNK
    nb = nh * nc
    qk_w = nh * GDN_DK

    def conv_heads(base):
        return jnp.stack([_silu(_causal_conv(x_ref, tail_scr, cw_ref, slice(base + h * GDN_DK, base + (h + 1) * GDN_DK), ts))
                          for h in range(nh)], axis=0)

    q = conv_heads(0)
    k = conv_heads(qk_w)
    v = conv_heads(2 * qk_w).reshape(nb, CHUNK, GDN_DV)
    q = (q * lax.rsqrt(jnp.sum(q * q, axis=-1, keepdims=True) + EPS) * (GDN_DK ** -0.5)).reshape(nb, CHUNK, GDN_DK)
    k = (k * lax.rsqrt(jnp.sum(k * k, axis=-1, keepdims=True) + EPS)).reshape(nb, CHUNK, GDN_DK)

    gcol = gcol_ref[...]
    grow = grow_ref[...]
    g_col = jnp.stack([gcol[:, 16 + h:17 + h] for h in range(nh)], axis=0).reshape(nb, CHUNK, 1)
    beta = jnp.stack([gcol[:, 24 + h:25 + h] for h in range(nh)], axis=0).reshape(nb, CHUNK, 1)
    g_row = jnp.stack([grow[:, c * CHUNK:(c + 1) * CHUNK] for c in range(nc)], axis=1).reshape(nb, 1, CHUNK)
    g_end = g_col[:, CHUNK - 1:CHUNK, :]

    tri = _tri(CHUNK)[None]
    tri_strict = _tri(CHUNK, strict=True)[None]
    eye = (lax.broadcasted_iota(I32, (CHUNK, CHUNK), 0) == lax.broadcasted_iota(I32, (CHUNK, CHUNK), 1)).astype(F32)[None]
    kb = k.astype(BF16)
    decay = jnp.exp(jnp.where(tri, g_col - g_row, -jnp.inf))
    qk_kk = _bdot_nt(jnp.concatenate([q, k], axis=1).astype(BF16), kb)
    a_mat = jnp.where(tri_strict, beta * qk_kk[:, CHUNK:] * decay, 0.0)
    p = -a_mat
    y = eye + p
    pb = p.astype(BF16)
    p = _bdot(pb, pb)
    for _ in range(4):
        py = _bdot(p.astype(BF16), jnp.concatenate([y, p], axis=2).astype(BF16))
        y = y + py[:, :, :CHUNK]
        p = py[:, :, CHUNK:]
    y = y + _bdot(p.astype(BF16), y.astype(BF16))
    eg = jnp.exp(g_col)
    rhs = jnp.concatenate([beta * v, (beta * eg) * k], axis=2).astype(BF16)
    sol = _bdot(y.astype(BF16), rhs)
    u = sol[:, :, :GDN_DV].reshape(nh, nc, CHUNK, GDN_DV)
    wq = jnp.concatenate([sol[:, :, GDN_DV:], q * eg], axis=1).astype(BF16).reshape(nh, nc, 2 * CHUNK, GDN_DK)
    qk = (qk_kk[:, :CHUNK] * decay).astype(BF16).reshape(nh, nc, CHUNK, CHUNK)
    k_dec = (k * jnp.exp(g_end - g_col)).astype(BF16).reshape(nh, nc, CHUNK, GDN_DK)
    g_tot = jnp.exp(g_end).reshape(nh, nc, 1, 1)

    s = s_scr[...]
    outs = []
    for c in range(nc):
        ws = _bdot(wq[:, c], s.astype(BF16))
        vb = (u[:, c] - ws[:, :CHUNK]).astype(BF16)
        outs.append(ws[:, CHUNK:] + _bdot(qk[:, c], vb))
        s = g_tot[:, c] * s + _bdot_tn(k_dec[:, c], vb)
    s_scr[...] = s

    o = jnp.stack(outs, axis=1).reshape(nh, ts, GDN_DV)
    on = _rms_rows(o) * gn_ref[...]
    for h in range(nh):
        cols = slice(h * GDN_DV, (h + 1) * GDN_DV)
        o_ref[:, cols] = (on[h] * _silu(z_ref[:, cols].astype(F32))).astype(o_ref.dtype)
    tail_scr[...] = x_ref[ts - 8:ts, :].astype(F32)


def _gdn(p, gcol, grow, conv_w, g_norm, batch, seq, ts=256):
    t = batch * seq
    ts = min(ts, seq)
    nt = seq // ts
    qkv_w = 3 * GDN_HEADS * GDN_DK
    v_w = GDN_HEADS * GDN_DV
    row = lambda b, i: b * nt + i
    return pl.pallas_call(
        functools.partial(_gdn_body, ts=ts),
        grid=(batch, nt),
        in_specs=[
            pl.BlockSpec((ts, qkv_w), lambda b, i: (row(b, i), 1)),
            pl.BlockSpec((ts, v_w), lambda b, i: (row(b, i), 6)),
            pl.BlockSpec((ts, LANES), lambda b, i: (row(b, i), 0)),
            pl.BlockSpec((8, ts), lambda b, i: (2, row(b, i))),
            pl.BlockSpec((CONV_WIDTH, qkv_w), lambda b, i: (0, 0)),
            pl.BlockSpec((1, GDN_DV), lambda b, i: (0, 0)),
        ],
        out_specs=pl.BlockSpec((ts, v_w), lambda b, i: (row(b, i), 0)),
        out_shape=jax.ShapeDtypeStruct((t, v_w), BF16),
        scratch_shapes=[
            pltpu.VMEM((8, qkv_w), F32),
            pltpu.VMEM((GDN_HEADS, GDN_DK, GDN_DV), F32),
        ],
        compiler_params=_cparams("parallel", "arbitrary"),
        name="gdn",
    )(p, p, gcol, grow, conv_w, g_norm.reshape(1, GDN_DV))


def _out_proj_body(x_ref, a_ref, b_ref, wa_ref, wb_ref, o_ref):
    o_ref[...] = x_ref[...] + _dot(a_ref[...], wa_ref[...]) + _dot(b_ref[...], wb_ref[...])


def _out_proj(x, a, b, wa, wb, tm=512):
    t, d = x.shape
    tm = min(tm, t)
    ka, kb = a.shape[1], b.shape[1]
    return pl.pallas_call(
        _out_proj_body,
        grid=(t // tm,),
        in_specs=[
            pl.BlockSpec((tm, d), lambda i: (i, 0)),
            pl.BlockSpec((tm, ka), lambda i: (i, 0)),
            pl.BlockSpec((tm, kb), lambda i: (i, 0)),
            pl.BlockSpec((ka, d), lambda i: (0, 0)),
            pl.BlockSpec((kb, d), lambda i: (0, 0)),
        ],
        out_specs=pl.BlockSpec((tm, d), lambda i: (i, 0)),
        out_shape=jax.ShapeDtypeStruct((t, d), F32),
        compiler_params=_cparams("parallel"),
        name="out_proj",
    )(x, a, b, wa, wb)


def _lane_table(entries):
    row = jnp.zeros((LANES,), F32)
    for start, vals in entries:
        row = lax.dynamic_update_slice(row, vals.astype(F32).reshape(-1), (start,))
    return row.reshape(1, LANES)


def _gate_weight(w_in, entries):
    w = jnp.zeros((w_in.shape[0], LANES), F32)
    for start, lo, hi in entries:
        w = lax.dynamic_update_slice(w, w_in[:, lo:hi].astype(F32), (0, start))
    return w


EVEN_GATE_LANES = ((0, 4, "id", False), (4, 8, "logsig", True), (16, 24, "softplus", True), (24, 32, "sigmoid", False))


def _even_mix(xf, batch, seq, gain, w_in, i_bias, f_bias, m_norm, conv_w, a_log, dt_bias, g_norm, w_out):
    w_big = jnp.concatenate([w_in[:, 0:3072], w_in[:, 3080:7176]], axis=1).astype(BF16)
    w_gate = _gate_weight(w_in, ((0, 3072, 3076), (4, 3076, 3080), (16, 7176, 7184), (24, 7184, 7192)))
    aux = jnp.concatenate([
        _lane_table(((0, i_bias), (4, f_bias), (16, dt_bias))),
        _lane_table(((16, -jnp.exp(a_log)),)),
        jnp.zeros((6, LANES), F32)], axis=0)
    p = _norm_matmul(xf, gain, w_big)
    gcol, grow = _gates(xf, gain, w_gate, aux, EVEN_GATE_LANES)
    hm = _mlstm(p, gcol, grow, m_norm, batch, seq)
    o = _gdn(p, gcol, grow, conv_w, g_norm, batch, seq)
    v_w = MLSTM_HEADS * MLSTM_DV
    return _out_proj(xf, hm, o, w_out[:v_w].astype(BF16), w_out[v_w:].astype(BF16))


ATT_TQ = 256


def _attn_body(q_ref, k0_ref, k1_ref, k2_ref, v0_ref, v1_ref, v2_ref, bias_ref, qg_ref, kg_ref, o_ref):
    i = pl.program_id(2)

    def head_norm(x):
        lo = lax.broadcasted_iota(I32, x.shape, 1) < ATT_DH
        sq = x * x
        ms_lo = jnp.sum(jnp.where(lo, sq, 0.0), axis=-1, keepdims=True)
        ms_hi = jnp.sum(jnp.where(lo, 0.0, sq), axis=-1, keepdims=True)
        return x * lax.rsqrt(jnp.where(lo, ms_lo, ms_hi) * (1.0 / ATT_DH) + EPS)

    qn = (head_norm(q_ref[...].astype(F32)) * qg_ref[...]).astype(BF16)
    kcat = jnp.concatenate([k0_ref[...], k1_ref[...], k2_ref[...]], axis=0).astype(F32)
    kn = (head_norm(kcat) * kg_ref[...]).astype(BF16)
    vcat = jnp.concatenate([v0_ref[...], v1_ref[...], v2_ref[...]], axis=0)
    tq = q_ref.shape[0]
    valid = lax.broadcasted_iota(I32, (tq, 3 * tq), 1) >= (2 - i) * tq
    outs = []
    for hh in range(2):
        cs = slice(hh * ATT_DH, (hh + 1) * ATT_DH)
        s = jnp.where(valid, _dot_nt(qn[:, cs], kn[:, cs]) + bias_ref[hh], -jnp.inf)
        p = jnp.exp(s - jnp.max(s, axis=-1, keepdims=True))
        l = jnp.sum(p, axis=-1, keepdims=True)
        outs.append(_dot(p.astype(BF16), vcat[:, cs]) / l)
    o_ref[...] = jnp.concatenate(outs, axis=1).astype(o_ref.dtype)


def _attn(p, bias, qg, kg, batch, seq, col0):
    t = batch * seq
    tq = ATT_TQ
    nt = seq // tq
    npair = ATT_HEADS // 2
    row = lambda b, i, back: b * nt + jnp.maximum(i - back, 0)
    kv_spec = lambda off, back: pl.BlockSpec((tq, LANES), lambda hp, b, i: (row(b, i, back), col0 + off + hp))
    return pl.pallas_call(
        _attn_body,
        grid=(npair, batch, nt),
        in_specs=[
            pl.BlockSpec((tq, LANES), lambda hp, b, i: (row(b, i, 0), col0 + hp)),
            kv_spec(npair, 2), kv_spec(npair, 1), kv_spec(npair, 0),
            kv_spec(2 * npair, 2), kv_spec(2 * npair, 1), kv_spec(2 * npair, 0),
            pl.BlockSpec((2, tq, 3 * tq), lambda hp, b, i: (hp, 0, 0)),
            pl.BlockSpec((1, LANES), lambda hp, b, i: (0, 0)),
            pl.BlockSpec((1, LANES), lambda hp, b, i: (0, 0)),
        ],
        out_specs=pl.BlockSpec((tq, LANES), lambda hp, b, i: (row(b, i, 0), hp)),
        out_shape=jax.ShapeDtypeStruct((t, ATT_HEADS * ATT_DH), BF16),
        compiler_params=_cparams("parallel", "parallel", "arbitrary"),
        name="band_attn",
    )(p, p, p, p, p, p, p, bias, qg, kg)


def _attn_bias_table(rel_bias):
    pad = ATT_LEFT_CHUNKS * CHUNK
    band = pad + CHUNK
    rel = jnp.arange(CHUNK)[:, None] + pad - jnp.arange(band)[None, :]
    rel_idx = jnp.clip(rel, -(CHUNK - 1), REL_MAX_PAST) + (CHUNK - 1)
    bias64 = rel_bias.astype(F32)[:, rel_idx]
    nq = ATT_TQ // CHUNK
    rows = [jnp.pad(bias64, ((0, 0), (0, 0), (a * CHUNK, (nq - 1 - a) * CHUNK)), constant_values=-jnp.inf)
            for a in range(nq)]
    return jnp.concatenate(rows, axis=1)


def _ssd_body(x_ref, z_ref, gcol_ref, grow_ref, cw_ref, cb_ref, dsk_ref, sn_ref, o_ref,
              tail_scr, xc_scr, y_scr, s_scr, *, ts):
    @pl.when(pl.program_id(1) == 0)
    def _():
        tail_scr[...] = jnp.zeros_like(tail_scr)
        s_scr[...] = jnp.zeros_like(s_scr)

    conv_w = x_ref.shape[1]
    for j in range(conv_w // LANES):
        cols = slice(j * LANES, (j + 1) * LANES)
        xc_scr[:, cols] = _silu(_causal_conv(x_ref, tail_scr, cw_ref, cols, ts, bias=cb_ref))

    nc, ng, npair = ts // CHUNK, SSD_GROUPS, SSD_HEADS // 2
    ppg = npair // ng
    nb = npair * nc
    gw = SSD_HPG * SSD_DH
    lo_half = lax.broadcasted_iota(I32, (1, LANES), 1) < SSD_DH
    gcol = gcol_ref[...]
    grow = grow_ref[...]

    def pair_cols(base):
        return jnp.stack([jnp.where(lo_half, gcol[:, base + p:base + p + 1], gcol[:, base + npair + p:base + npair + p + 1])
                          for p in range(npair)], axis=0).reshape(nb, CHUNK, LANES)

    dt = pair_cols(0)
    a_col = pair_cols(SSD_HEADS)
    a_row = jnp.stack([jnp.concatenate([grow[0:npair, c * CHUNK:(c + 1) * CHUNK], grow[npair:, c * CHUNK:(c + 1) * CHUNK]], axis=1)
                       for c in range(nc)], axis=1).reshape(nb, 1, LANES)
    a_end = a_col[:, CHUNK - 1:CHUNK, :]
    xs = jnp.stack([xc_scr[:, p * LANES:(p + 1) * LANES] for p in range(npair)], axis=0).reshape(nb, CHUNK, LANES)
    dsk = jnp.stack([dsk_ref[:, p * LANES:(p + 1) * LANES] for p in range(npair)], axis=0)
    dsk = jnp.broadcast_to(dsk[:, None], (npair, nc, 1, LANES)).reshape(nb, 1, LANES)
    bm = jnp.stack([xc_scr[:, SSD_INNER + g * SSD_STATE:SSD_INNER + (g + 1) * SSD_STATE] for g in range(ng)], axis=0)
    cm = jnp.stack([xc_scr[:, SSD_INNER + SSD_BC_W + g * SSD_STATE:SSD_INNER + SSD_BC_W + (g + 1) * SSD_STATE]
                    for g in range(ng)], axis=0)
    bm = bm.astype(BF16).reshape(ng * nc, CHUNK, SSD_STATE)
    cm = cm.astype(BF16).reshape(ng * nc, CHUNK, SSD_STATE)

    tri2 = (lax.broadcasted_iota(I32, (CHUNK, LANES), 0) >= (lax.broadcasted_iota(I32, (CHUNK, LANES), 1) & (SSD_DH - 1)))[None]
    decay = jnp.exp(jnp.where(tri2, a_col - a_row, -jnp.inf))
    cb = _bdot_nt(cm, bm)
    cb2 = jnp.concatenate([cb, cb], axis=2).reshape(ng, 1, nc, CHUNK, LANES)
    cb2 = jnp.broadcast_to(cb2, (ng, ppg, nc, CHUNK, LANES)).reshape(nb, CHUNK, LANES)
    xd = xs * dt
    rhs = jnp.concatenate([jnp.where(lo_half, xd, 0.0), jnp.where(lo_half, 0.0, xd)], axis=1).astype(BF16)
    y_in = (_bdot((cb2 * decay).astype(BF16), rhs) + dsk * xs).reshape(ng, ppg, nc, CHUNK, LANES)
    ea = jnp.exp(a_col).reshape(ng, ppg, nc, CHUNK, LANES)
    dxd = (xd * jnp.exp(a_end - a_col)).astype(BF16).reshape(ng, ppg, nc, CHUNK, LANES)
    cdec = jnp.exp(a_end).reshape(ng, ppg, nc, 1, LANES)
    bm = bm.reshape(ng, nc, CHUNK, SSD_STATE)
    cm = cm.reshape(ng, nc, CHUNK, SSD_STATE)

    s = s_scr[...]
    for c in range(nc):
        rows = slice(c * CHUNK, (c + 1) * CHUNK)
        y_off = _bdot(cm[:, c], s.astype(BF16))
        for j in range(ppg):
            yj = y_in[:, j, c] + y_off[:, :, j * LANES:(j + 1) * LANES] * ea[:, j, c]
            for g in range(ng):
                y_scr[rows, (g * ppg + j) * LANES:(g * ppg + j + 1) * LANES] = yj[g]
        dx = jnp.concatenate([dxd[:, j, c] for j in range(ppg)], axis=2)
        cd = jnp.concatenate([cdec[:, j, c] for j in range(ppg)], axis=2)
        s = s * cd + _bdot_tn(bm[:, c], dx)
    s_scr[...] = s

    for g in range(SSD_GROUPS):
        cols = slice(g * gw, (g + 1) * gw)
        yg = y_scr[:, cols] * _silu(z_ref[:, cols].astype(F32))
        o_ref[:, cols] = (_rms_rows(yg) * sn_ref[:, cols]).astype(o_ref.dtype)
    tail_scr[...] = x_ref[ts - 8:ts, :].astype(F32)


def _ssd(p, gcol, grow, conv_w, conv_b, d_skip, s_norm, batch, seq, ts=256):
    t = batch * seq
    ts = min(ts, seq)
    nt = seq // ts
    cw = SSD_INNER + 2 * SSD_BC_W
    row = lambda b, i: b * nt + i
    return pl.pallas_call(
        functools.partial(_ssd_body, ts=ts),
        grid=(batch, nt),
        in_specs=[
            pl.BlockSpec((ts, cw), lambda b, i: (row(b, i), 0)),
            pl.BlockSpec((ts, SSD_INNER), lambda b, i: (row(b, i), 3)),
            pl.BlockSpec((ts, LANES), lambda b, i: (row(b, i), 0)),
            pl.BlockSpec((SSD_HEADS, ts), lambda b, i: (1, row(b, i))),
            pl.BlockSpec((CONV_WIDTH, cw), lambda b, i: (0, 0)),
            pl.BlockSpec((1, cw), lambda b, i: (0, 0)),
            pl.BlockSpec((1, SSD_INNER), lambda b, i: (0, 0)),
            pl.BlockSpec((1, SSD_INNER), lambda b, i: (0, 0)),
        ],
        out_specs=pl.BlockSpec((ts, SSD_INNER), lambda b, i: (row(b, i), 0)),
        out_shape=jax.ShapeDtypeStruct((t, SSD_INNER), BF16),
        scratch_shapes=[
            pltpu.VMEM((8, cw), F32),
            pltpu.VMEM((ts, cw), F32),
            pltpu.VMEM((ts, SSD_INNER), F32),
            pltpu.VMEM((SSD_GROUPS, SSD_STATE, SSD_HPG * SSD_DH), F32),
        ],
        compiler_params=_cparams("parallel", "arbitrary"),
        name="ssd",
    )(p, p, gcol, grow, conv_w, conv_b.reshape(1, cw), jnp.repeat(d_skip, SSD_DH).reshape(1, SSD_INNER),
      s_norm.reshape(1, SSD_INNER))


ODD_GATE_LANES = ((0, 32, "softplus", False), (32, 64, "softplus", True))


def _odd_mix(xf, batch, seq, gain, w_in, q_gain, k_gain, rel_bias, conv_w, conv_b, a_log, dt_bias, d_skip, s_norm, w_out):
    w_big = jnp.concatenate([w_in[:, 5120:8192], w_in[:, 0:3072], w_in[:, 3072:5120]], axis=1).astype(BF16)
    perm = jnp.concatenate([jnp.arange(0, SSD_HEADS, 2), jnp.arange(1, SSD_HEADS, 2)])
    w_dt = w_in[:, 8192:8224][:, perm]
    w_gate = _gate_weight(jnp.concatenate([w_dt, w_dt], axis=1), ((0, 0, 2 * SSD_HEADS),))
    aux = jnp.concatenate([
        _lane_table(((0, dt_bias[perm]), (SSD_HEADS, dt_bias[perm]))),
        _lane_table(((0, jnp.ones((SSD_HEADS,), F32)), (SSD_HEADS, -jnp.exp(a_log[perm])))),
        jnp.zeros((6, LANES), F32)], axis=0)
    p = _norm_matmul(xf, gain, w_big)
    gcol, grow = _gates(xf, gain, w_gate, aux, ODD_GATE_LANES)
    qg = jnp.tile(q_gain.astype(F32) * ATT_DH ** -0.5, 2).reshape(1, LANES)
    kg = jnp.tile(k_gain.astype(F32), 2).reshape(1, LANES)
    attn = _attn(p, _attn_bias_table(rel_bias), qg, kg, batch, seq, col0=3072 // LANES)
    y = _ssd(p, gcol, grow, conv_w, conv_b, d_skip, s_norm, batch, seq)
    a_w = ATT_HEADS * ATT_DH
    return _out_proj(xf, attn, y, w_out[:a_w].astype(BF16), w_out[a_w:].astype(BF16))


ROW_W = D_MODEL + LANES
MOE_TM = 1024
DMA_ROWS = 256


def _router_body(x_ref, g_ref, w_ref, b_ref, rows_ref, info_ref, cnt_ref, carry_scr):
    @pl.when(pl.program_id(0) == 0)
    def _():
        carry_scr[...] = jnp.zeros_like(carry_scr)

    h = _rms_rows(x_ref[...]) * g_ref[...]
    logits = jnp.dot(h, w_ref[...], precision=HIGHEST, preferred_element_type=F32) + b_ref[...]
    tm = logits.shape[0]
    lane = lax.broadcasted_iota(I32, logits.shape, 1).astype(F32)
    ninf = -jnp.inf

    def first_max(v):
        m = jnp.max(v, axis=-1, keepdims=True)
        return m, jnp.min(jnp.where(v == m, lane, float(LANES)), axis=-1, keepdims=True)

    glog = jnp.where(lane < MOE_GROUPS, logits, ninf)
    gmax, gidx = first_max(glog)
    grp_w = 1.0 / jnp.sum(jnp.exp(glog - gmax), axis=-1, keepdims=True)
    e_lo = MOE_GROUPS + gidx * MOE_EPG
    in_grp = (lane >= e_lo) & (lane < e_lo + MOE_EPG)
    elog = jnp.where(in_grp, logits, ninf)
    v1, i1 = first_max(elog)
    v2, i2 = first_max(jnp.where(lane == i1, ninf, elog))
    e21 = jnp.exp(v2 - v1)
    w1 = 1.0 / (1.0 + e21)
    comb = grp_w * (jnp.where(lane == i1, w1, 0.0) + jnp.where(lane == i2, e21 * w1, 0.0))
    local = jnp.zeros_like(comb)
    for g in range(MOE_GROUPS):
        local = local + pltpu.roll(comb, LANES - MOE_GROUPS - g * MOE_EPG, 1)
    local = jnp.where(lane < MOE_EPG, local, 0.0)
    rows_ref[...] = jnp.concatenate([h, local], axis=1)

    onehot = jnp.where(lane == gidx, 1.0, 0.0)
    lower = jnp.where(_tri(tm, strict=True), 1.0, 0.0).astype(BF16)
    before = _dot(lower, onehot.astype(BF16)) + carry_scr[0:1, :]
    rank = jnp.sum(onehot * before, axis=-1, keepdims=True)
    carry = carry_scr[0:1, :] + jnp.sum(onehot, axis=0, keepdims=True)
    carry_scr[...] = jnp.broadcast_to(carry, carry_scr.shape)
    cnt_ref[...] = jnp.broadcast_to(carry, cnt_ref.shape)
    info = jnp.where(lane == 0, gidx, jnp.where(lane == 1, rank, 0.0))
    info_ref[...] = info.T[0:8, :]


def _router(x, gain, w, b, tm=256):
    t, d = x.shape
    tm = min(tm, t)
    return pl.pallas_call(
        _router_body,
        grid=(t // tm,),
        in_specs=[
            pl.BlockSpec((tm, d), lambda i: (i, 0)),
            pl.BlockSpec((1, d), lambda i: (0, 0)),
            pl.BlockSpec((d, LANES), lambda i: (0, 0)),
            pl.BlockSpec((1, LANES), lambda i: (0, 0)),
        ],
        out_specs=[
            pl.BlockSpec((tm, ROW_W), lambda i: (i, 0)),
            pl.BlockSpec((8, tm), lambda i: (0, i)),
            pl.BlockSpec((8, LANES), lambda i: (0, 0)),
        ],
        out_shape=[
            jax.ShapeDtypeStruct((t, ROW_W), F32),
            jax.ShapeDtypeStruct((8, t), F32),
            jax.ShapeDtypeStruct((8, LANES), F32),
        ],
        scratch_shapes=[pltpu.VMEM((8, LANES), F32)],
        compiler_params=_cparams("arbitrary"),
        name="moe_router",
    )(x, gain.reshape(1, d), w, b)


def _dispatch_body(pos_ref, rows_ref, init_ref, out_ref, sem):
    del init_ref
    base = pl.program_id(0) * DMA_ROWS

    def row_copy(r):
        return pltpu.make_async_copy(rows_ref.at[pl.ds(r, 1)], out_ref.at[pl.ds(pos_ref[base + r], 1)], sem)

    def start(r, carry):
        row_copy(r).start()
        return carry

    def wait(r, carry):
        row_copy(r).wait()
        return carry

    lax.fori_loop(0, DMA_ROWS, start, 0, unroll=8)
    lax.fori_loop(0, DMA_ROWS, wait, 0, unroll=8)


def _dispatch(pos, rows, n_sorted):
    t, w = rows.shape
    return pl.pallas_call(
        _dispatch_body,
        grid_spec=pltpu.PrefetchScalarGridSpec(
            num_scalar_prefetch=1,
            grid=(t // DMA_ROWS,),
            in_specs=[
                pl.BlockSpec((DMA_ROWS, w), lambda i, pos: (i, 0)),
                pl.BlockSpec(memory_space=pl.ANY),
            ],
            out_specs=pl.BlockSpec(memory_space=pl.ANY),
            scratch_shapes=[pltpu.SemaphoreType.DMA(())],
        ),
        out_shape=jax.ShapeDtypeStruct((n_sorted, w), rows.dtype),
        input_output_aliases={2: 0},
        compiler_params=_cparams("arbitrary"),
        name="moe_dispatch",
    )(pos, rows, jnp.zeros((n_sorted, w), rows.dtype))


def _experts_body(tile_group_ref, n_used_ref, xs_ref, wg_ref, wu_ref, wd_ref, y_ref, x_scr, acc_scr):
    del tile_group_ref
    i = pl.program_id(0)
    e = pl.program_id(1)

    @pl.when(i < n_used_ref[0])
    def _():
        @pl.when(e == 0)
        def _():
            x_scr[...] = xs_ref[:, :D_MODEL].astype(BF16)
            acc_scr[...] = jnp.zeros_like(acc_scr)

        x = x_scr[...]
        a = _dot(x, wg_ref[...].astype(BF16))
        u = _dot(x, wu_ref[...].astype(BF16))
        comb = xs_ref[:, D_MODEL:]
        lane = lax.broadcasted_iota(I32, comb.shape, 1)
        ce = jnp.sum(jnp.where(lane == e, comb, 0.0), axis=-1, keepdims=True)
        act = (_silu(a) * u * ce).astype(BF16)
        acc_scr[...] += _dot(act, wd_ref[...].astype(BF16))

        @pl.when(e == MOE_EPG - 1)
        def _():
            y_ref[...] = acc_scr[...]

    @pl.when((i >= n_used_ref[0]) & (e == 0))
    def _():
        y_ref[...] = jnp.zeros_like(y_ref)


def _experts(tile_group, n_used, xs, w_gate, w_up, w_down):
    n_sorted = xs.shape[0]
    tm = MOE_TM
    w_in_spec = pl.BlockSpec((None, None, D_MODEL, D_EXPERT), lambda i, e, tg, nu: (tg[i], e, 0, 0))
    return pl.pallas_call(
        _experts_body,
        grid_spec=pltpu.PrefetchScalarGridSpec(
            num_scalar_prefetch=2,
            grid=(n_sorted // tm, MOE_EPG),
            in_specs=[
                pl.BlockSpec((tm, ROW_W), lambda i, e, tg, nu: (i, 0)),
                w_in_spec,
                w_in_spec,
                pl.BlockSpec((None, None, D_EXPERT, D_MODEL), lambda i, e, tg, nu: (tg[i], e, 0, 0)),
            ],
            out_specs=pl.BlockSpec((tm, D_MODEL), lambda i, e, tg, nu: (i, 0)),
            scratch_shapes=[pltpu.VMEM((tm, D_MODEL), BF16), pltpu.VMEM((tm, D_MODEL), F32)],
        ),
        out_shape=jax.ShapeDtypeStruct((n_sorted, D_MODEL), F32),
        compiler_params=_cparams("parallel", "arbitrary"),
        name="moe_experts",
    )(tile_group, n_used, xs, w_gate, w_up, w_down)


def _combine_body(pos_ref, x_ref, y_ref, o_ref, buf, sem):
    base = pl.program_id(0) * DMA_ROWS

    def row_copy(r):
        return pltpu.make_async_copy(y_ref.at[pl.ds(pos_ref[base + r], 1)], buf.at[pl.ds(r, 1)], sem)

    def start(r, carry):
        row_copy(r).start()
        return carry

    def wait(r, carry):
        row_copy(r).wait()
        return carry

    lax.fori_loop(0, DMA_ROWS, start, 0, unroll=8)
    lax.fori_loop(0, DMA_ROWS, wait, 0, unroll=8)
    o_ref[...] = x_ref[...] + buf[...]


def _combine(pos, x, y):
    t, d = x.shape
    return pl.pallas_call(
        _combine_body,
        grid_spec=pltpu.PrefetchScalarGridSpec(
            num_scalar_prefetch=1,
            grid=(t // DMA_ROWS,),
            in_specs=[
                pl.BlockSpec((DMA_ROWS, d), lambda i, pos: (i, 0)),
                pl.BlockSpec(memory_space=pl.ANY),
            ],
            out_specs=pl.BlockSpec((DMA_ROWS, d), lambda i, pos: (i, 0)),
            scratch_shapes=[pltpu.VMEM((DMA_ROWS, d), F32), pltpu.SemaphoreType.DMA(())],
        ),
        out_shape=jax.ShapeDtypeStruct((t, d), F32),
        compiler_params=_cparams("arbitrary"),
        name="moe_combine",
    )(pos, x, y)


def _moe(xf, gain, wg, bg, we, be, w_gate, w_up, w_down):
    t = xf.shape[0]
    n_exp = MOE_GROUPS * MOE_EPG
    w_r = jnp.zeros((D_MODEL, LANES), F32)
    w_r = lax.dynamic_update_slice(w_r, wg.astype(F32), (0, 0))
    w_r = lax.dynamic_update_slice(w_r, we.astype(F32), (0, MOE_GROUPS))
    b_r = _lane_table(((0, bg), (MOE_GROUPS, be)))
    rows, info, cnt = _router(xf, gain, w_r, b_r)

    gidx = info[0].astype(I32)
    rank = info[1].astype(I32)
    counts = cnt[0, :MOE_GROUPS].astype(I32)
    padded = ((counts + MOE_TM - 1) // MOE_TM) * MOE_TM
    ends = jnp.cumsum(padded)
    starts = ends - padded
    pos = starts[gidx] + rank
    n_sorted = t + MOE_GROUPS * MOE_TM
    n_tiles = n_sorted // MOE_TM
    tile_start = jnp.arange(n_tiles, dtype=I32) * MOE_TM
    tile_group = jnp.minimum(jnp.sum(tile_start[:, None] >= ends[None, :], axis=1), MOE_GROUPS - 1).astype(I32)
    n_used = (ends[-1:] // MOE_TM).astype(I32)

    xs = _dispatch(pos, rows, n_sorted)
    ys = _experts(tile_group, n_used, xs, w_gate, w_up, w_down)
    return _combine(pos, xf, ys)


def kernel(x, norm_mix, norm_ffn, even_w_in, mlstm_i_bias, mlstm_f_bias, mlstm_norm, gdn_conv, gdn_A_log, gdn_dt_bias,
           gdn_norm, even_w_out, odd_w_in, att_q_norm, att_k_norm, att_rel_bias, ssd_conv, ssd_conv_bias, ssd_A_log,
           ssd_dt_bias, ssd_D, ssd_norm, odd_w_out, router_group_w, router_group_b, router_expert_w, router_expert_b,
           moe_w_gate, moe_w_up, moe_w_down):
    batch, seq, d = x.shape
    xf = x.reshape(batch * seq, d)
    depth = norm_mix.shape[0]
    for layer in range(depth):
        i = layer // 2
        if layer % 2 == 0:
            xf = _even_mix(xf, batch, seq, norm_mix[layer], even_w_in[i], mlstm_i_bias[i], mlstm_f_bias[i], mlstm_norm[i],
                           gdn_conv[i], gdn_A_log[i], gdn_dt_bias[i], gdn_norm[i], even_w_out[i])
        else:
            xf = _odd_mix(xf, batch, seq, norm_mix[layer], odd_w_in[i], att_q_norm[i], att_k_norm[i], att_rel_bias[i],
                          ssd_conv[i], ssd_conv_bias[i], ssd_A_log[i], ssd_dt_bias[i], ssd_D[i], ssd_norm[i], odd_w_out[i])
        xf = _moe(xf, norm_ffn[layer], router_group_w[layer], router_group_b[layer], router_expert_w[layer],
                  router_expert_b[layer], moe_w_gate[layer], moe_w_up[layer], moe_w_down[layer])
    return xf.reshape(batch, seq, d)
```

```python
import functools

import jax
import jax.numpy as jnp
from jax import lax
from jax.experimental import pallas as pl
from jax.experimental.pallas import tpu as pltpu

F32 = jnp.float32
BF16 = jnp.bfloat16
I32 = jnp.int32
HIGHEST = lax.Precision.HIGHEST

D_MODEL = 1024
CHUNK = 64
CONV_WIDTH = 4
EPS = 1e-6
LANES = 128

MLSTM_HEADS, MLSTM_DQK, MLSTM_DV = 4, 128, 256
GDN_HEADS, GDN_DK, GDN_DV = 8, 128, 128
ATT_HEADS, ATT_DH, ATT_LEFT_CHUNKS, REL_MAX_PAST = 16, 64, 8, 256
SSD_HEADS, SSD_DH, SSD_STATE, SSD_GROUPS = 32, 64, 128, 4
SSD_HPG = SSD_HEADS // SSD_GROUPS
SSD_INNER = SSD_HEADS * SSD_DH
SSD_BC_W = SSD_GROUPS * SSD_STATE
MOE_GROUPS, MOE_EPG, D_EXPERT = 4, 8, 256

VMEM_LIMIT = 48 * 1024 * 1024


def _cparams(*sem):
    return pltpu.CompilerParams(dimension_semantics=sem, vmem_limit_bytes=VMEM_LIMIT)


def _softplus(x):
    return jnp.maximum(x, 0.0) + jnp.log1p(jnp.exp(-jnp.abs(x)))


def _sigmoid(x):
    return 1.0 / (1.0 + jnp.exp(-x))


def _silu(x):
    return x * _sigmoid(x)


def _rms_rows(x):
    return x * lax.rsqrt(jnp.mean(x * x, axis=-1, keepdims=True) + EPS)


def _dot(a, b):
    return jnp.dot(a, b, preferred_element_type=F32)


def _dot_nt(a, b):
    return lax.dot_general(a, b, (((1,), (1,)), ((), ())), preferred_element_type=F32)


def _dot_tn(a, b):
    return lax.dot_general(a, b, (((0,), (0,)), ((), ())), preferred_element_type=F32)


def _bdot(a, b):
    return lax.dot_general(a, b, (((2,), (1,)), ((0,), (0,))), preferred_element_type=F32)


def _bdot_nt(a, b):
    return lax.dot_general(a, b, (((2,), (2,)), ((0,), (0,))), preferred_element_type=F32)


def _bdot_tn(a, b):
    return lax.dot_general(a, b, (((1,), (1,)), ((0,), (0,))), preferred_element_type=F32)


def _tri(n, strict=False):
    r = lax.broadcasted_iota(I32, (n, n), 0)
    c = lax.broadcasted_iota(I32, (n, n), 1)
    return (r > c) if strict else (r >= c)


def _split_bf16(x):
    hi = x.astype(BF16)
    return hi, (x - hi.astype(F32)).astype(BF16)


def _pair_head_norm(x):
    lo = lax.broadcasted_iota(I32, x.shape, 1) < (LANES // 2)
    sq = x * x
    ms_lo = jnp.sum(jnp.where(lo, sq, 0.0), axis=-1, keepdims=True)
    ms_hi = jnp.sum(jnp.where(lo, 0.0, sq), axis=-1, keepdims=True)
    return x * lax.rsqrt(jnp.where(lo, ms_lo, ms_hi) * (2.0 / LANES) + EPS)


def _norm_matmul_body(x_ref, g_ref, w_ref, wgh_ref, wgl_ref, post_ref, o_ref, gate_ref, h_scr, *, norm_tiles):
    j = pl.program_id(1)

    @pl.when(j == 0)
    def _():
        hi, lo = _split_bf16(_rms_rows(x_ref[...]) * g_ref[...])
        h_scr[...] = hi
        gate_ref[...] = _dot(hi, wgh_ref[...]) + _dot(lo, wgh_ref[...]) + _dot(hi, wgl_ref[...])

    acc = _dot(h_scr[...], w_ref[...])
    if not norm_tiles:
        o_ref[...] = acc.astype(o_ref.dtype)
        return
    is_norm = functools.reduce(jnp.logical_or, [j == tile for tile in norm_tiles])

    @pl.when(is_norm)
    def _():
        for s in range(acc.shape[1] // LANES):
            cols = slice(s * LANES, (s + 1) * LANES)
            o_ref[:, cols] = (_pair_head_norm(acc[:, cols]) * post_ref[:, cols]).astype(o_ref.dtype)

    @pl.when(jnp.logical_not(is_norm))
    def _():
        o_ref[...] = acc.astype(o_ref.dtype)


def _norm_matmul(x, gain, w, w_gate, post=None, norm_tiles=(), tm=1024, tn=1024):
    t, d = x.shape
    n = w.shape[1]
    tm = min(tm, t)
    wg_hi, wg_lo = _split_bf16(w_gate)
    if post is None:
        post = jnp.ones((1, n), F32)
    return pl.pallas_call(
        functools.partial(_norm_matmul_body, norm_tiles=tuple(norm_tiles)),
        grid=(t // tm, n // tn),
        in_specs=[
            pl.BlockSpec((tm, d), lambda i, j: (i, 0)),
            pl.BlockSpec((1, d), lambda i, j: (0, 0)),
            pl.BlockSpec((d, tn), lambda i, j: (0, j)),
            pl.BlockSpec((d, LANES), lambda i, j: (0, 0)),
            pl.BlockSpec((d, LANES), lambda i, j: (0, 0)),
            pl.BlockSpec((1, tn), lambda i, j: (0, j)),
        ],
        out_specs=[
            pl.BlockSpec((tm, tn), lambda i, j: (i, j)),
            pl.BlockSpec((tm, LANES), lambda i, j: (i, 0)),
        ],
        out_shape=[jax.ShapeDtypeStruct((t, n), BF16), jax.ShapeDtypeStruct((t, LANES), F32)],
        scratch_shapes=[pltpu.VMEM((tm, d), BF16)],
        compiler_params=_cparams("parallel", "arbitrary"),
        name="norm_matmul",
    )(x, gain.reshape(1, d), w, wg_hi, wg_lo, post)


def _gates_body(pre_ref, aux_ref, col_ref, row_ref, *, lanes):
    pre = pre_ref[...] + aux_ref[0:1, :]
    scale = aux_ref[1:2, :]
    lane = lax.broadcasted_iota(I32, pre.shape, 1)
    sp = _softplus(pre)
    vals = jnp.zeros_like(pre)
    cum_mask = jnp.zeros(pre.shape, jnp.bool_)
    for lo, hi, kind, cum in lanes:
        m = (lane >= lo) & (lane < hi)
        if kind == "id":
            v = pre
        elif kind == "logsig":
            v = pre - sp
        elif kind == "softplus":
            v = scale * sp
        else:
            v = _sigmoid(pre)
        vals = jnp.where(m, v, vals)
        if cum:
            cum_mask = cum_mask | m
    pos = lax.broadcasted_iota(I32, pre.shape, 0) & (CHUNK - 1)
    c = vals
    k = 1
    while k < CHUNK:
        c = c + jnp.where(pos >= k, pltpu.roll(c, k, 0), 0.0)
        k *= 2
    out = jnp.where(cum_mask, c, vals)
    col_ref[...] = out
    row_ref[...] = out.T


def _gates(pre, aux, lanes, tm=2048):
    t = pre.shape[0]
    tm = min(tm, t)
    return pl.pallas_call(
        functools.partial(_gates_body, lanes=lanes),
        grid=(t // tm,),
        in_specs=[
            pl.BlockSpec((tm, LANES), lambda i: (i, 0)),
            pl.BlockSpec((8, LANES), lambda i: (0, 0)),
        ],
        out_specs=[
            pl.BlockSpec((tm, LANES), lambda i: (i, 0)),
            pl.BlockSpec((LANES, tm), lambda i: (0, i)),
        ],
        out_shape=[jax.ShapeDtypeStruct((t, LANES), F32), jax.ShapeDtypeStruct((LANES, t), F32)],
        compiler_params=_cparams("parallel"),
        name="gates",
    )(pre, aux)


def _mlstm_body(q_ref, k_ref, v_ref, og_ref, gcol_ref, grow_ref, mn_ref, o_ref, c_scr, n_scr, m_scr, *, ts):
    @pl.when(pl.program_id(1) == 0)
    def _():
        c_scr[...] = jnp.zeros_like(c_scr)
        n_scr[...] = jnp.zeros_like(n_scr)
        m_scr[...] = jnp.zeros_like(m_scr)

    scale = MLSTM_DQK ** -0.5
    nh, nc = MLSTM_HEADS, ts // CHUNK
    nb = nh * nc
    q = jnp.stack([q_ref[:, h * MLSTM_DQK:(h + 1) * MLSTM_DQK] for h in range(nh)], axis=0).reshape(nb, CHUNK, MLSTM_DQK)
    k = jnp.stack([k_ref[:, h * MLSTM_DQK:(h + 1) * MLSTM_DQK] for h in range(nh)], axis=0).reshape(nb, CHUNK, MLSTM_DQK)
    v = jnp.stack([v_ref[:, h * MLSTM_DV:(h + 1) * MLSTM_DV] for h in range(nh)], axis=0).reshape(nb, CHUNK, MLSTM_DV)
    gcol = gcol_ref[...]
    grow = grow_ref[...]
    li_col = jnp.stack([gcol[:, h:h + 1] for h in range(nh)], axis=0).reshape(nb, CHUNK, 1)
    b_col = jnp.stack([gcol[:, nh + h:nh + h + 1] for h in range(nh)], axis=0).reshape(nb, CHUNK, 1)
    li_row = jnp.stack([grow[0:nh, c * CHUNK:(c + 1) * CHUNK] for c in range(nc)], axis=1).reshape(nb, 1, CHUNK)
    b_row = jnp.stack([grow[nh:2 * nh, c * CHUNK:(c + 1) * CHUNK] for c in range(nc)], axis=1).reshape(nb, 1, CHUNK)

    b_end = b_col[:, CHUNK - 1:CHUNK, :]
    g_end = b_end - b_col + li_col
    m_loc = jnp.max(g_end, axis=1, keepdims=True)
    kw = k.astype(F32) * jnp.exp(g_end - m_loc)
    upd_c = _bdot_tn(kw.astype(BF16), v).reshape(nh, nc, MLSTM_DQK, MLSTM_DV)
    upd_n = jnp.sum(kw, axis=1, keepdims=True).reshape(nh, nc, 1, MLSTM_DQK)
    log_d = jnp.where(_tri(CHUNK)[None], b_col - b_row + li_row, -jnp.inf)
    max_d = jnp.max(log_d, axis=2, keepdims=True)
    s_raw = _bdot_nt(q, k)

    b_end4 = b_end.reshape(nh, nc, 1, 1)
    m_loc4 = m_loc.reshape(nh, nc, 1, 1)
    m_prev = m_scr[...][:, 0:1, 0:1]
    m_prevs, s_olds, s_upds = [], [], []
    for c in range(nc):
        m_prevs.append(m_prev)
        m_new = jnp.maximum(b_end4[:, c] + m_prev, m_loc4[:, c])
        s_olds.append(jnp.exp(b_end4[:, c] + m_prev - m_new))
        s_upds.append(jnp.exp(m_loc4[:, c] - m_new))
        m_prev = m_new
    m_scr[...] = jnp.broadcast_to(m_prev, m_scr.shape)

    inter_log = b_col + jnp.stack(m_prevs, axis=1).reshape(nb, 1, 1)
    m_t = jnp.maximum(inter_log, max_d)
    w_inter = jnp.exp(inter_log - m_t)
    scores = s_raw * (scale * jnp.exp(log_d - m_t))
    num = _bdot(scores.astype(BF16), v)
    den = jnp.sum(scores, axis=2, keepdims=True)

    q4 = q.reshape(nh, nc, CHUNK, MLSTM_DQK)
    c_state = c_scr[...]
    n_state = n_scr[...][:, 0:1, :]
    qcs, qns = [], []
    for c in range(nc):
        qcs.append(_bdot(q4[:, c], c_state.astype(BF16)))
        qns.append(jnp.sum(q4[:, c].astype(F32) * n_state, axis=2, keepdims=True))
        c_state = s_olds[c] * c_state + s_upds[c] * upd_c[:, c]
        n_state = s_olds[c] * n_state + s_upds[c] * upd_n[:, c]
    c_scr[...] = c_state
    n_scr[...] = jnp.broadcast_to(n_state, n_scr.shape)

    num = num + w_inter * (jnp.stack(qcs, axis=1).reshape(nb, CHUNK, MLSTM_DV) * scale)
    den = den + w_inter * (jnp.stack(qns, axis=1).reshape(nb, CHUNK, 1) * scale)
    hout = num / jnp.maximum(jnp.abs(den), jnp.exp(-m_t))
    hn = _rms_rows(hout).reshape(nh, ts, MLSTM_DV)
    for h in range(nh):
        cols = slice(h * MLSTM_DV, (h + 1) * MLSTM_DV)
        o_ref[:, cols] = (hn[h] * mn_ref[:, cols] * _sigmoid(og_ref[:, cols].astype(F32))).astype(o_ref.dtype)


def _mlstm(p, gcol, grow, m_norm, batch, seq, ts=256):
    t = batch * seq
    ts = min(ts, seq)
    nt = seq // ts
    qk_w = MLSTM_HEADS * MLSTM_DQK
    v_w = MLSTM_HEADS * MLSTM_DV
    row = lambda b, i: b * nt + i
    return pl.pallas_call(
        functools.partial(_mlstm_body, ts=ts),
        grid=(batch, nt),
        in_specs=[
            pl.BlockSpec((ts, qk_w), lambda b, i: (row(b, i), 0)),
            pl.BlockSpec((ts, qk_w), lambda b, i: (row(b, i), 1)),
            pl.BlockSpec((ts, v_w), lambda b, i: (row(b, i), 1)),
            pl.BlockSpec((ts, v_w), lambda b, i: (row(b, i), 2)),
            pl.BlockSpec((ts, LANES), lambda b, i: (row(b, i), 0)),
            pl.BlockSpec((8, ts), lambda b, i: (0, row(b, i))),
            pl.BlockSpec((1, v_w), lambda b, i: (0, 0)),
        ],
        out_specs=pl.BlockSpec((ts, v_w), lambda b, i: (row(b, i), 0)),
        out_shape=jax.ShapeDtypeStruct((t, v_w), BF16),
        scratch_shapes=[
            pltpu.VMEM((MLSTM_HEADS, MLSTM_DQK, MLSTM_DV), F32),
            pltpu.VMEM((MLSTM_HEADS, 8, MLSTM_DQK), F32),
            pltpu.VMEM((MLSTM_HEADS, 8, LANES), F32),
        ],
        compiler_params=_cparams("parallel", "arbitrary"),
        name="mlstm",
    )(p, p, p, p, gcol, grow, m_norm.reshape(1, v_w))


def _causal_conv(x_ref, tail_ref, w_ref, cols, ts, bias=None):
    xp = jnp.concatenate([tail_ref[:, cols], x_ref[:, cols].astype(F32)], axis=0)
    y = w_ref[CONV_WIDTH - 1:CONV_WIDTH, cols] * xp[8:, :]
    for j in range(CONV_WIDTH - 1):
        y = y + w_ref[j:j + 1, cols] * pltpu.roll(xp, CONV_WIDTH - 1 - j, 0)[8:, :]
    if bias is not None:
        y = y + bias[:, cols]
    return y


def _gdn_body(x_ref, z_ref, gcol_ref, grow_ref, cw_ref, gn_ref, o_ref, tail_scr, s_scr, *, ts):
    @pl.when(pl.program_id(1) == 0)
    def _():
        tail_scr[...] = jnp.zeros_like(tail_scr)
        s_scr[...] = jnp.zeros_like(s_scr)

    nh, nc = GDN_HEADS, ts // CHUNK
    nb = nh * nc
    qk_w = nh * GDN_DK

    def conv_heads(base):
        return jnp.stack([_silu(_causal_conv(x_ref, tail_scr, cw_ref, slice(base + h * GDN_DK, base + (h + 1) * GDN_DK), ts))
                          for h in range(nh)], axis=0)

    q = conv_heads(0)
    k = conv_heads(qk_w)
    v = conv_heads(2 * qk_w).reshape(nb, CHUNK, GDN_DV)
    q = (q * lax.rsqrt(jnp.sum(q * q, axis=-1, keepdims=True) + EPS) * (GDN_DK ** -0.5)).reshape(nb, CHUNK, GDN_DK)
    k = (k * lax.rsqrt(jnp.sum(k * k, axis=-1, keepdims=True) + EPS)).reshape(nb, CHUNK, GDN_DK)

    gcol = gcol_ref[...]
    grow = grow_ref[...]
    g_col = jnp.stack([gcol[:, 16 + h:17 + h] for h in range(nh)], axis=0).reshape(nb, CHUNK, 1)
    beta = jnp.stack([gcol[:, 24 + h:25 + h] for h in range(nh)], axis=0).reshape(nb, CHUNK, 1)
    g_row = jnp.stack([grow[:, c * CHUNK:(c + 1) * CHUNK] for c in range(nc)], axis=1).reshape(nb, 1, CHUNK)
    g_end = g_col[:, CHUNK - 1:CHUNK, :]

    tri = _tri(CHUNK)[None]
    tri_strict = _tri(CHUNK, strict=True)[None]
    eye = (lax.broadcasted_iota(I32, (CHUNK, CHUNK), 0) == lax.broadcasted_iota(I32, (CHUNK, CHUNK), 1)).astype(F32)[None]
    kb = k.astype(BF16)
    decay = jnp.exp(jnp.where(tri, g_col - g_row, -jnp.inf))
    qk_kk = _bdot_nt(jnp.concatenate([q, k], axis=1).astype(BF16), kb)
    a_mat = jnp.where(tri_strict, beta * qk_kk[:, CHUNK:] * decay, 0.0)
    p = -a_mat
    y = eye + p
    pb = p.astype(BF16)
    p = _bdot(pb, pb)
    for _ in range(4):
        py = _bdot(p.astype(BF16), jnp.concatenate([y, p], axis=2).astype(BF16))
        y = y + py[:, :, :CHUNK]
        p = py[:, :, CHUNK:]
    y = y + _bdot(p.astype(BF16), y.astype(BF16))
    eg = jnp.exp(g_col)
    rhs = jnp.concatenate([beta * v, (beta * eg) * k], axis=2).astype(BF16)
    sol = _bdot(y.astype(BF16), rhs)
    u = sol[:, :, :GDN_DV].reshape(nh, nc, CHUNK, GDN_DV)
    wq = jnp.concatenate([sol[:, :, GDN_DV:], q * eg], axis=1).astype(BF16).reshape(nh, nc, 2 * CHUNK, GDN_DK)
    qk = (qk_kk[:, :CHUNK] * decay).astype(BF16).reshape(nh, nc, CHUNK, CHUNK)
    k_dec = (k * jnp.exp(g_end - g_col)).astype(BF16).reshape(nh, nc, CHUNK, GDN_DK)
    g_tot = jnp.exp(g_end).reshape(nh, nc, 1, 1)

    s = s_scr[...]
    outs = []
    for c in range(nc):
        ws = _bdot(wq[:, c], s.astype(BF16))
        vb = (u[:, c] - ws[:, :CHUNK]).astype(BF16)
        outs.append(ws[:, CHUNK:] + _bdot(qk[:, c], vb))
        s = g_tot[:, c] * s + _bdot_tn(k_dec[:, c], vb)
    s_scr[...] = s

    o = jnp.stack(outs, axis=1).reshape(nh, ts, GDN_DV)
    on = _rms_rows(o) * gn_ref[...]
    for h in range(nh):
        cols = slice(h * GDN_DV, (h + 1) * GDN_DV)
        o_ref[:, cols] = (on[h] * _silu(z_ref[:, cols].astype(F32))).astype(o_ref.dtype)
    tail_scr[...] = x_ref[ts - 8:ts, :].astype(F32)


def _gdn(p, gcol, grow, conv_w, g_norm, batch, seq, ts=256):
    t = batch * seq
    ts = min(ts, seq)
    nt = seq // ts
    qkv_w = 3 * GDN_HEADS * GDN_DK
    v_w = GDN_HEADS * GDN_DV
    row = lambda b, i: b * nt + i
    return pl.pallas_call(
        functools.partial(_gdn_body, ts=ts),
        grid=(batch, nt),
        in_specs=[
            pl.BlockSpec((ts, qkv_w), lambda b, i: (row(b, i), 1)),
            pl.BlockSpec((ts, v_w), lambda b, i: (row(b, i), 6)),
            pl.BlockSpec((ts, LANES), lambda b, i: (row(b, i), 0)),
            pl.BlockSpec((8, ts), lambda b, i: (2, row(b, i))),
            pl.BlockSpec((CONV_WIDTH, qkv_w), lambda b, i: (0, 0)),
            pl.BlockSpec((1, GDN_DV), lambda b, i: (0, 0)),
        ],
        out_specs=pl.BlockSpec((ts, v_w), lambda b, i: (row(b, i), 0)),
        out_shape=jax.ShapeDtypeStruct((t, v_w), BF16),
        scratch_shapes=[
            pltpu.VMEM((8, qkv_w), F32),
            pltpu.VMEM((GDN_HEADS, GDN_DK, GDN_DV), F32),
        ],
        compiler_params=_cparams("parallel", "arbitrary"),
        name="gdn",
    )(p, p, gcol, grow, conv_w, g_norm.reshape(1, GDN_DV))


def _out_proj_body(x_ref, a_ref, b_ref, wa_ref, wb_ref, o_ref):
    o_ref[...] = x_ref[...] + _dot(a_ref[...], wa_ref[...]) + _dot(b_ref[...], wb_ref[...])


def _out_proj(x, a, b, wa, wb, tm=512):
    t, d = x.shape
    tm = min(tm, t)
    ka, kb = a.shape[1], b.shape[1]
    return pl.pallas_call(
        _out_proj_body,
        grid=(t // tm,),
        in_specs=[
            pl.BlockSpec((tm, d), lambda i: (i, 0)),
            pl.BlockSpec((tm, ka), lambda i: (i, 0)),
            pl.BlockSpec((tm, kb), lambda i: (i, 0)),
            pl.BlockSpec((ka, d), lambda i: (0, 0)),
            pl.BlockSpec((kb, d), lambda i: (0, 0)),
        ],
        out_specs=pl.BlockSpec((tm, d), lambda i: (i, 0)),
        out_shape=jax.ShapeDtypeStruct((t, d), F32),
        compiler_params=_cparams("parallel"),
        name="out_proj",
    )(x, a, b, wa, wb)


def _lane_table(entries):
    row = jnp.zeros((LANES,), F32)
    for start, vals in entries:
        row = lax.dynamic_update_slice(row, vals.astype(F32).reshape(-1), (start,))
    return row.reshape(1, LANES)


def _gate_weight(w_in, entries):
    w = jnp.zeros((w_in.shape[0], LANES), F32)
    for start, lo, hi in entries:
        w = lax.dynamic_update_slice(w, w_in[:, lo:hi].astype(F32), (0, start))
    return w


EVEN_GATE_LANES = ((0, 4, "id", False), (4, 8, "logsig", True), (16, 24, "softplus", True), (24, 32, "sigmoid", False))


def _even_mix(xf, batch, seq, gain, w_in, i_bias, f_bias, m_norm, conv_w, a_log, dt_bias, g_norm, w_out):
    w_big = jnp.concatenate([w_in[:, 0:3072], w_in[:, 3080:7176]], axis=1).astype(BF16)
    w_gate = _gate_weight(w_in, ((0, 3072, 3076), (4, 3076, 3080), (16, 7176, 7184), (24, 7184, 7192)))
    aux = jnp.concatenate([
        _lane_table(((0, i_bias), (4, f_bias), (16, dt_bias))),
        _lane_table(((16, -jnp.exp(a_log)),)),
        jnp.zeros((6, LANES), F32)], axis=0)
    p, pre = _norm_matmul(xf, gain, w_big, w_gate)
    gcol, grow = _gates(pre, aux, EVEN_GATE_LANES)
    hm = _mlstm(p, gcol, grow, m_norm, batch, seq)
    o = _gdn(p, gcol, grow, conv_w, g_norm, batch, seq)
    v_w = MLSTM_HEADS * MLSTM_DV
    return _out_proj(xf, hm, o, w_out[:v_w].astype(BF16), w_out[v_w:].astype(BF16))


ATT_TQ = 256


def _attn_body(q_ref, k0_ref, k1_ref, k2_ref, v0_ref, v1_ref, v2_ref, bias_ref, o_ref):
    i = pl.program_id(1)
    tq = q_ref.shape[0]
    lo_half = lax.broadcasted_iota(I32, (1, LANES), 1) < ATT_DH
    k_refs = (k0_ref, k1_ref, k2_ref)
    v_refs = (v0_ref, v1_ref, v2_ref)
    for hp in range(ATT_HEADS // 2):
        cols = slice(hp * LANES, (hp + 1) * LANES)
        q = q_ref[:, cols]
        ks = [r[:, cols] for r in k_refs]
        vs = [r[:, cols] for r in v_refs]
        outs = []
        for hh in range(2):
            qm = jnp.where(lo_half if hh == 0 else jnp.logical_not(lo_half), q, jnp.zeros_like(q))
            s = [_dot_nt(qm, ks[j]) + bias_ref[2 * hp + hh, :, j * tq:(j + 1) * tq] for j in range(3)]
            s[0] = jnp.where(i >= 2, s[0], -jnp.inf)
            s[1] = jnp.where(i >= 1, s[1], -jnp.inf)
            m = functools.reduce(jnp.maximum, [jnp.max(sj, axis=-1, keepdims=True) for sj in s])
            p = [jnp.exp(sj - m) for sj in s]
            l = functools.reduce(jnp.add, [jnp.sum(pj, axis=-1, keepdims=True) for pj in p])
            o = functools.reduce(jnp.add, [_dot(p[j].astype(BF16), vs[j]) for j in range(3)])
            outs.append(o / l)
        o_ref[:, cols] = jnp.where(lo_half, outs[0], outs[1]).astype(o_ref.dtype)


def _attn(p, bias, batch, seq, col0):
    t = batch * seq
    tq = ATT_TQ
    nt = seq // tq
    w = ATT_HEADS * ATT_DH
    row = lambda b, i, back: b * nt + jnp.maximum(i - back, 0)
    kv_spec = lambda off, back: pl.BlockSpec((tq, w), lambda b, i: (row(b, i, back), col0 + off))
    return pl.pallas_call(
        _attn_body,
        grid=(batch, nt),
        in_specs=[
            pl.BlockSpec((tq, w), lambda b, i: (row(b, i, 0), col0)),
            kv_spec(1, 2), kv_spec(1, 1), kv_spec(1, 0),
            kv_spec(2, 2), kv_spec(2, 1), kv_spec(2, 0),
            pl.BlockSpec((ATT_HEADS, tq, 3 * tq), lambda b, i: (0, 0, 0)),
        ],
        out_specs=pl.BlockSpec((tq, w), lambda b, i: (row(b, i, 0), 0)),
        out_shape=jax.ShapeDtypeStruct((t, w), BF16),
        compiler_params=_cparams("parallel", "arbitrary"),
        name="band_attn",
    )(p, p, p, p, p, p, p, bias)


def _attn_bias_table(rel_bias):
    pad = ATT_LEFT_CHUNKS * CHUNK
    band = pad + CHUNK
    rel = jnp.arange(CHUNK)[:, None] + pad - jnp.arange(band)[None, :]
    rel_idx = jnp.clip(rel, -(CHUNK - 1), REL_MAX_PAST) + (CHUNK - 1)
    onehot = (rel_idx[None, :, :] == jnp.arange(rel_bias.shape[1])[:, None, None]).astype(F32)
    bias64 = jnp.einsum("hr,rlj->hlj", rel_bias.astype(F32), onehot, precision=HIGHEST)
    nq = ATT_TQ // CHUNK
    rows = [jnp.pad(bias64, ((0, 0), (0, 0), (a * CHUNK, (nq - 1 - a) * CHUNK)), constant_values=-jnp.inf)
            for a in range(nq)]
    return jnp.concatenate(rows, axis=1)


def _ssd_body(x_ref, z_ref, gcol_ref, grow_ref, cw_ref, cb_ref, dsk_ref, sn_ref, o_ref,
              tail_scr, xc_scr, y_scr, s_scr, *, ts):
    @pl.when(pl.program_id(1) == 0)
    def _():
        tail_scr[...] = jnp.zeros_like(tail_scr)
        s_scr[...] = jnp.zeros_like(s_scr)

    conv_w = x_ref.shape[1]
    for j in range(conv_w // LANES):
        cols = slice(j * LANES, (j + 1) * LANES)
        xc_scr[:, cols] = _silu(_causal_conv(x_ref, tail_scr, cw_ref, cols, ts, bias=cb_ref))

    nc, ng, npair = ts // CHUNK, SSD_GROUPS, SSD_HEADS // 2
    ppg = npair // ng
    nb = npair * nc
    gw = SSD_HPG * SSD_DH
    lo_half = lax.broadcasted_iota(I32, (1, LANES), 1) < SSD_DH
    gcol = gcol_ref[...]
    grow = grow_ref[...]

    def pair_cols(base):
        return jnp.stack([jnp.where(lo_half, gcol[:, base + p:base + p + 1], gcol[:, base + npair + p:base + npair + p + 1])
                          for p in range(npair)], axis=0).reshape(nb, CHUNK, LANES)

    dt = pair_cols(0)
    a_col = pair_cols(SSD_HEADS)
    a_row = jnp.stack([jnp.concatenate([grow[0:npair, c * CHUNK:(c + 1) * CHUNK], grow[npair:, c * CHUNK:(c + 1) * CHUNK]], axis=1)
                       for c in range(nc)], axis=1).reshape(nb, 1, LANES)
    a_end = a_col[:, CHUNK - 1:CHUNK, :]
    xs = jnp.stack([xc_scr[:, p * LANES:(p + 1) * LANES] for p in range(npair)], axis=0).reshape(nb, CHUNK, LANES)
    dsk = jnp.stack([dsk_ref[:, p * LANES:(p + 1) * LANES] for p in range(npair)], axis=0)
    dsk = jnp.broadcast_to(dsk[:, None], (npair, nc, 1, LANES)).reshape(nb, 1, LANES)
    bm = jnp.stack([xc_scr[:, SSD_INNER + g * SSD_STATE:SSD_INNER + (g + 1) * SSD_STATE] for g in range(ng)], axis=0)
    cm = jnp.stack([xc_scr[:, SSD_INNER + SSD_BC_W + g * SSD_STATE:SSD_INNER + SSD_BC_W + (g + 1) * SSD_STATE]
                    for g in range(ng)], axis=0)
    bm = bm.astype(BF16).reshape(ng * nc, CHUNK, SSD_STATE)
    cm = cm.astype(BF16).reshape(ng * nc, CHUNK, SSD_STATE)

    tri2 = (lax.broadcasted_iota(I32, (CHUNK, LANES), 0) >= (lax.broadcasted_iota(I32, (CHUNK, LANES), 1) & (SSD_DH - 1)))[None]
    decay = jnp.exp(jnp.where(tri2, a_col - a_row, -jnp.inf))
    cb = _bdot_nt(cm, bm)
    cb2 = jnp.concatenate([cb, cb], axis=2).reshape(ng, 1, nc, CHUNK, LANES)
    cb2 = jnp.broadcast_to(cb2, (ng, ppg, nc, CHUNK, LANES)).reshape(nb, CHUNK, LANES)
    xd = xs * dt
    rhs = jnp.concatenate([jnp.where(lo_half, xd, 0.0), jnp.where(lo_half, 0.0, xd)], axis=1).astype(BF16)
    y_in = (_bdot((cb2 * decay).astype(BF16), rhs) + dsk * xs).reshape(ng, ppg, nc, CHUNK, LANES)
    ea = jnp.exp(a_col).reshape(ng, ppg, nc, CHUNK, LANES)
    dxd = (xd * jnp.exp(a_end - a_col)).astype(BF16).reshape(ng, ppg, nc, CHUNK, LANES)
    cdec = jnp.exp(a_end).reshape(ng, ppg, nc, 1, LANES)
    bm = bm.reshape(ng, nc, CHUNK, SSD_STATE)
    cm = cm.reshape(ng, nc, CHUNK, SSD_STATE)

    s = s_scr[...]
    for c in range(nc):
        rows = slice(c * CHUNK, (c + 1) * CHUNK)
        y_off = _bdot(cm[:, c], s.astype(BF16))
        for j in range(ppg):
            yj = y_in[:, j, c] + y_off[:, :, j * LANES:(j + 1) * LANES] * ea[:, j, c]
            for g in range(ng):
                y_scr[rows, (g * ppg + j) * LANES:(g * ppg + j + 1) * LANES] = yj[g]
        dx = jnp.concatenate([dxd[:, j, c] for j in range(ppg)], axis=2)
        cd = jnp.concatenate([cdec[:, j, c] for j in range(ppg)], axis=2)
        s = s * cd + _bdot_tn(bm[:, c], dx)
    s_scr[...] = s

    for g in range(SSD_GROUPS):
        cols = slice(g * gw, (g + 1) * gw)
        yg = y_scr[:, cols] * _silu(z_ref[:, cols].astype(F32))
        o_ref[:, cols] = (_rms_rows(yg) * sn_ref[:, cols]).astype(o_ref.dtype)
    tail_scr[...] = x_ref[ts - 8:ts, :].astype(F32)


def _ssd(p, gcol, grow, conv_w, conv_b, d_skip, s_norm, batch, seq, ts=256):
    t = batch * seq
    ts = min(ts, seq)
    nt = seq // ts
    cw = SSD_INNER + 2 * SSD_BC_W
    row = lambda b, i: b * nt + i
    return pl.pallas_call(
        functools.partial(_ssd_body, ts=ts),
        grid=(batch, nt),
        in_specs=[
            pl.BlockSpec((ts, cw), lambda b, i: (row(b, i), 0)),
            pl.BlockSpec((ts, SSD_INNER), lambda b, i: (row(b, i), 3)),
            pl.BlockSpec((ts, LANES), lambda b, i: (row(b, i), 0)),
            pl.BlockSpec((SSD_HEADS, ts), lambda b, i: (1, row(b, i))),
            pl.BlockSpec((CONV_WIDTH, cw), lambda b, i: (0, 0)),
            pl.BlockSpec((1, cw), lambda b, i: (0, 0)),
            pl.BlockSpec((1, SSD_INNER), lambda b, i: (0, 0)),
            pl.BlockSpec((1, SSD_INNER), lambda b, i: (0, 0)),
        ],
        out_specs=pl.BlockSpec((ts, SSD_INNER), lambda b, i: (row(b, i), 0)),
        out_shape=jax.ShapeDtypeStruct((t, SSD_INNER), BF16),
        scratch_shapes=[
            pltpu.VMEM((8, cw), F32),
            pltpu.VMEM((ts, cw), F32),
            pltpu.VMEM((ts, SSD_INNER), F32),
            pltpu.VMEM((SSD_GROUPS, SSD_STATE, SSD_HPG * SSD_DH), F32),
        ],
        compiler_params=_cparams("parallel", "arbitrary"),
        name="ssd",
    )(p, p, gcol, grow, conv_w, conv_b.reshape(1, cw), jnp.repeat(d_skip, SSD_DH).reshape(1, SSD_INNER),
      s_norm.reshape(1, SSD_INNER))


ODD_GATE_LANES = ((0, 32, "softplus", False), (32, 64, "softplus", True))


def _odd_mix(xf, batch, seq, gain, w_in, q_gain, k_gain, rel_bias, conv_w, conv_b, a_log, dt_bias, d_skip, s_norm, w_out):
    w_big = jnp.concatenate([w_in[:, 5120:8192], w_in[:, 0:3072], w_in[:, 3072:5120]], axis=1).astype(BF16)
    even_odd = lambda a: jnp.concatenate([a[..., 0::2], a[..., 1::2]], axis=-1)
    w_dt = even_odd(w_in[:, 8192:8224])
    w_gate = _gate_weight(jnp.concatenate([w_dt, w_dt], axis=1), ((0, 0, 2 * SSD_HEADS),))
    aux = jnp.concatenate([
        _lane_table(((0, even_odd(dt_bias)), (SSD_HEADS, even_odd(dt_bias)))),
        _lane_table(((0, jnp.ones((SSD_HEADS,), F32)), (SSD_HEADS, -jnp.exp(even_odd(a_log))))),
        jnp.zeros((6, LANES), F32)], axis=0)
    a_w = ATT_HEADS * ATT_DH
    post = jnp.ones((1, w_big.shape[1]), F32)
    post = lax.dynamic_update_slice(post, jnp.tile(q_gain.astype(F32) * ATT_DH ** -0.5, ATT_HEADS).reshape(1, a_w), (0, 3072))
    post = lax.dynamic_update_slice(post, jnp.tile(k_gain.astype(F32), ATT_HEADS).reshape(1, a_w), (0, 3072 + a_w))
    p, pre = _norm_matmul(xf, gain, w_big, w_gate, post=post, norm_tiles=(3, 4))
    gcol, grow = _gates(pre, aux, ODD_GATE_LANES)
    attn = _attn(p, _attn_bias_table(rel_bias), batch, seq, col0=3072 // a_w)
    y = _ssd(p, gcol, grow, conv_w, conv_b, d_skip, s_norm, batch, seq)
    return _out_proj(xf, attn, y, w_out[:a_w].astype(BF16), w_out[a_w:].astype(BF16))


ROW_W = D_MODEL + LANES
MOE_TM = 512
DMA_ROWS = 256


def _router_body(x_ref, g_ref, wh_ref, wl_ref, b_ref, rows_ref, info_ref, cnt_ref, carry_scr):
    @pl.when(pl.program_id(0) == 0)
    def _():
        carry_scr[...] = jnp.zeros_like(carry_scr)

    h = _rms_rows(x_ref[...]) * g_ref[...]
    h_hi, h_lo = _split_bf16(h)
    logits = _dot(h_hi, wh_ref[...]) + _dot(h_lo, wh_ref[...]) + _dot(h_hi, wl_ref[...]) + b_ref[...]
    tm = logits.shape[0]
    lane = lax.broadcasted_iota(I32, logits.shape, 1).astype(F32)
    ninf = -jnp.inf

    def first_max(v):
        m = jnp.max(v, axis=-1, keepdims=True)
        return m, jnp.min(jnp.where(v == m, lane, float(LANES)), axis=-1, keepdims=True)

    glog = jnp.where(lane < MOE_GROUPS, logits, ninf)
    gmax, gidx = first_max(glog)
    grp_w = 1.0 / jnp.sum(jnp.exp(glog - gmax), axis=-1, keepdims=True)
    e_lo = MOE_GROUPS + gidx * MOE_EPG
    in_grp = (lane >= e_lo) & (lane < e_lo + MOE_EPG)
    elog = jnp.where(in_grp, logits, ninf)
    v1, i1 = first_max(elog)
    v2, i2 = first_max(jnp.where(lane == i1, ninf, elog))
    e21 = jnp.exp(v2 - v1)
    w1 = 1.0 / (1.0 + e21)
    comb = grp_w * (jnp.where(lane == i1, w1, 0.0) + jnp.where(lane == i2, e21 * w1, 0.0))
    local = jnp.zeros_like(comb)
    for g in range(MOE_GROUPS):
        local = local + pltpu.roll(comb, LANES - MOE_GROUPS - g * MOE_EPG, 1)
    local = jnp.where(lane < MOE_EPG, local, 0.0)
    rows_ref[...] = jnp.concatenate([h, local], axis=1)

    onehot = jnp.where(lane == gidx, 1.0, 0.0)
    lower = jnp.where(_tri(tm, strict=True), 1.0, 0.0).astype(BF16)
    before = _dot(lower, onehot.astype(BF16)) + carry_scr[0:1, :]
    rank = jnp.sum(onehot * before, axis=-1, keepdims=True)
    carry = carry_scr[0:1, :] + jnp.sum(onehot, axis=0, keepdims=True)
    carry_scr[...] = jnp.broadcast_to(carry, carry_scr.shape)
    cnt_ref[...] = jnp.broadcast_to(carry, cnt_ref.shape)
    info = jnp.where(lane == 0, gidx, jnp.where(lane == 1, rank, 0.0))
    info_ref[...] = info.T[0:8, :]


def _router(x, gain, w, b, tm=256):
    t, d = x.shape
    tm = min(tm, t)
    w_hi, w_lo = _split_bf16(w)
    return pl.pallas_call(
        _router_body,
        grid=(t // tm,),
        in_specs=[
            pl.BlockSpec((tm, d), lambda i: (i, 0)),
            pl.BlockSpec((1, d), lambda i: (0, 0)),
            pl.BlockSpec((d, LANES), lambda i: (0, 0)),
            pl.BlockSpec((d, LANES), lambda i: (0, 0)),
            pl.BlockSpec((1, LANES), lambda i: (0, 0)),
        ],
        out_specs=[
            pl.BlockSpec((tm, ROW_W), lambda i: (i, 0)),
            pl.BlockSpec((8, tm), lambda i: (0, i)),
            pl.BlockSpec((8, LANES), lambda i: (0, 0)),
        ],
        out_shape=[
            jax.ShapeDtypeStruct((t, ROW_W), F32),
            jax.ShapeDtypeStruct((8, t), F32),
            jax.ShapeDtypeStruct((8, LANES), F32),
        ],
        scratch_shapes=[pltpu.VMEM((8, LANES), F32)],
        compiler_params=_cparams("arbitrary"),
        name="moe_router",
    )(x, gain.reshape(1, d), w_hi, w_lo, b)


def _dispatch_body(pos_ref, rows_ref, init_ref, out_ref, sem):
    del init_ref
    base = pl.program_id(0) * DMA_ROWS

    def row_copy(r):
        return pltpu.make_async_copy(rows_ref.at[pl.ds(r, 1)], out_ref.at[pl.ds(pos_ref[base + r], 1)], sem)

    def start(r, carry):
        row_copy(r).start()
        return carry

    def wait(r, carry):
        row_copy(r).wait()
        return carry

    lax.fori_loop(0, DMA_ROWS, start, 0, unroll=8)
    lax.fori_loop(0, DMA_ROWS, wait, 0, unroll=8)


def _dispatch(pos, rows, n_sorted):
    t, w = rows.shape
    return pl.pallas_call(
        _dispatch_body,
        grid_spec=pltpu.PrefetchScalarGridSpec(
            num_scalar_prefetch=1,
            grid=(t // DMA_ROWS,),
            in_specs=[
                pl.BlockSpec((DMA_ROWS, w), lambda i, pos: (i, 0)),
                pl.BlockSpec(memory_space=pl.ANY),
            ],
            out_specs=pl.BlockSpec(memory_space=pl.ANY),
            scratch_shapes=[pltpu.SemaphoreType.DMA(())],
        ),
        out_shape=jax.ShapeDtypeStruct((n_sorted, w), rows.dtype),
        input_output_aliases={2: 0},
        compiler_params=_cparams("arbitrary"),
        name="moe_dispatch",
    )(pos, rows, jnp.zeros((n_sorted, w), rows.dtype))


def _experts_body(tile_group_ref, n_used_ref, xs_ref, wg_ref, wu_ref, wd_ref, y_ref, act_scr):
    del tile_group_ref
    i = pl.program_id(0)

    @pl.when(i < n_used_ref[0])
    def _():
        x = xs_ref[:, :D_MODEL].astype(BF16)
        comb = xs_ref[:, D_MODEL:]
        for e in range(MOE_EPG):
            a = _dot(x, wg_ref[e])
            u = _dot(x, wu_ref[e])
            act_scr[:, e * D_EXPERT:(e + 1) * D_EXPERT] = (_silu(a) * u * comb[:, e:e + 1]).astype(BF16)
        y_ref[...] = _dot(act_scr[...], wd_ref[...].reshape(MOE_EPG * D_EXPERT, D_MODEL))

    @pl.when(i >= n_used_ref[0])
    def _():
        y_ref[...] = jnp.zeros_like(y_ref)


def _experts(tile_group, n_used, xs, w_gate, w_up, w_down):
    n_sorted = xs.shape[0]
    tm = MOE_TM
    w_in_spec = pl.BlockSpec((None, MOE_EPG, D_MODEL, D_EXPERT), lambda i, tg, nu: (tg[i], 0, 0, 0))
    return pl.pallas_call(
        _experts_body,
        grid_spec=pltpu.PrefetchScalarGridSpec(
            num_scalar_prefetch=2,
            grid=(n_sorted // tm,),
            in_specs=[
                pl.BlockSpec((tm, ROW_W), lambda i, tg, nu: (i, 0)),
                w_in_spec,
                w_in_spec,
                pl.BlockSpec((None, MOE_EPG, D_EXPERT, D_MODEL), lambda i, tg, nu: (tg[i], 0, 0, 0)),
            ],
            out_specs=pl.BlockSpec((tm, D_MODEL), lambda i, tg, nu: (i, 0)),
            scratch_shapes=[pltpu.VMEM((tm, MOE_EPG * D_EXPERT), BF16)],
        ),
        out_shape=jax.ShapeDtypeStruct((n_sorted, D_MODEL), F32),
        compiler_params=_cparams("arbitrary"),
        name="moe_experts",
    )(tile_group, n_used, xs, w_gate, w_up, w_down)


def _combine_body(pos_ref, x_ref, y_ref, o_ref, buf, sem):
    base = pl.program_id(0) * DMA_ROWS

    def row_copy(r):
        return pltpu.make_async_copy(y_ref.at[pl.ds(pos_ref[base + r], 1)], buf.at[pl.ds(r, 1)], sem)

    def start(r, carry):
        row_copy(r).start()
        return carry

    def wait(r, carry):
        row_copy(r).wait()
        return carry

    lax.fori_loop(0, DMA_ROWS, start, 0, unroll=8)
    lax.fori_loop(0, DMA_ROWS, wait, 0, unroll=8)
    o_ref[...] = x_ref[...] + buf[...]


def _combine(pos, x, y):
    t, d = x.shape
    return pl.pallas_call(
        _combine_body,
        grid_spec=pltpu.PrefetchScalarGridSpec(
            num_scalar_prefetch=1,
            grid=(t // DMA_ROWS,),
            in_specs=[
                pl.BlockSpec((DMA_ROWS, d), lambda i, pos: (i, 0)),
                pl.BlockSpec(memory_space=pl.ANY),
            ],
            out_specs=pl.BlockSpec((DMA_ROWS, d), lambda i, pos: (i, 0)),
            scratch_shapes=[pltpu.VMEM((DMA_ROWS, d), F32), pltpu.SemaphoreType.DMA(())],
        ),
        out_shape=jax.ShapeDtypeStruct((t, d), F32),
        compiler_params=_cparams("arbitrary"),
        name="moe_combine",
    )(pos, x, y)


def _moe(xf, gain, wg, bg, we, be, w_gate, w_up, w_down):
    t = xf.shape[0]
    w_r = jnp.zeros((D_MODEL, LANES), F32)
    w_r = lax.dynamic_update_slice(w_r, wg.astype(F32), (0, 0))
    w_r = lax.dynamic_update_slice(w_r, we.astype(F32), (0, MOE_GROUPS))
    b_r = _lane_table(((0, bg), (MOE_GROUPS, be)))
    rows, info, cnt = _router(xf, gain, w_r, b_r)

    gidx = info[0].astype(I32)
    rank = info[1].astype(I32)
    counts = cnt[0, :MOE_GROUPS].astype(I32)
    padded = ((counts + MOE_TM - 1) // MOE_TM) * MOE_TM
    ends = jnp.cumsum(padded)
    starts = ends - padded
    pos = rank + sum(jnp.where(gidx == g, starts[g], 0) for g in range(MOE_GROUPS))
    n_sorted = t + MOE_GROUPS * MOE_TM
    n_tiles = n_sorted // MOE_TM
    tile_start = jnp.arange(n_tiles, dtype=I32) * MOE_TM
    tile_group = jnp.minimum(jnp.sum(tile_start[:, None] >= ends[None, :], axis=1), MOE_GROUPS - 1).astype(I32)
    n_used = (ends[-1:] // MOE_TM).astype(I32)

    xs = _dispatch(pos, rows, n_sorted)
    ys = _experts(tile_group, n_used, xs, w_gate.astype(BF16), w_up.astype(BF16), w_down.astype(BF16))
    return _combine(pos, xf, ys)


def kernel(x, norm_mix, norm_ffn, even_w_in, mlstm_i_bias, mlstm_f_bias, mlstm_norm, gdn_conv, gdn_A_log, gdn_dt_bias,
           gdn_norm, even_w_out, odd_w_in, att_q_norm, att_k_norm, att_rel_bias, ssd_conv, ssd_conv_bias, ssd_A_log,
           ssd_dt_bias, ssd_D, ssd_norm, odd_w_out, router_group_w, router_group_b, router_expert_w, router_expert_b,
           moe_w_gate, moe_w_up, moe_w_down):
    batch, seq, d = x.shape
    xf = x.reshape(batch * seq, d)
    depth = norm_mix.shape[0]
    for layer in range(depth):
        i = layer // 2
        if layer % 2 == 0:
            xf = _even_mix(xf, batch, seq, norm_mix[layer], even_w_in[i], mlstm_i_bias[i], mlstm_f_bias[i], mlstm_norm[i],
                           gdn_conv[i], gdn_A_log[i], gdn_dt_bias[i], gdn_norm[i], even_w_out[i])
        else:
            xf = _odd_mix(xf, batch, seq, norm_mix[layer], odd_w_in[i], att_q_norm[i], att_k_norm[i], att_rel_bias[i],
                          ssd_conv[i], ssd_conv_bias[i], ssd_A_log[i], ssd_dt_bias[i], ssd_D[i], ssd_norm[i], odd_w_out[i])
        xf = _moe(xf, norm_ffn[layer], router_group_w[layer], router_group_b[layer], router_expert_w[layer],
                  router_expert_b[layer], moe_w_gate[layer], moe_w_up[layer], moe_w_down[layer])
    return xf.reshape(batch, seq, d)
```

```python
import functools

import jax
import jax.numpy as jnp
from jax import lax
from jax.experimental import pallas as pl
from jax.experimental.pallas import tpu as pltpu

F32 = jnp.float32
BF16 = jnp.bfloat16
I32 = jnp.int32
HIGHEST = lax.Precision.HIGHEST

D_MODEL = 1024
CHUNK = 64
CONV_WIDTH = 4
EPS = 1e-6
LANES = 128

MLSTM_HEADS, MLSTM_DQK, MLSTM_DV = 4, 128, 256
GDN_HEADS, GDN_DK, GDN_DV = 8, 128, 128
ATT_HEADS, ATT_DH, ATT_LEFT_CHUNKS, REL_MAX_PAST = 16, 64, 8, 256
SSD_HEADS, SSD_DH, SSD_STATE, SSD_GROUPS = 32, 64, 128, 4
SSD_HPG = SSD_HEADS // SSD_GROUPS
SSD_INNER = SSD_HEADS * SSD_DH
SSD_BC_W = SSD_GROUPS * SSD_STATE
MOE_GROUPS, MOE_EPG, D_EXPERT = 4, 8, 256

VMEM_LIMIT = 48 * 1024 * 1024


def _cparams(*sem):
    return pltpu.CompilerParams(dimension_semantics=sem, vmem_limit_bytes=VMEM_LIMIT)


def _softplus(x):
    return jnp.maximum(x, 0.0) + jnp.log1p(jnp.exp(-jnp.abs(x)))


def _sigmoid(x):
    return 1.0 / (1.0 + jnp.exp(-x))


def _silu(x):
    return x * _sigmoid(x)


def _rms_rows(x):
    return x * lax.rsqrt(jnp.mean(x * x, axis=-1, keepdims=True) + EPS)


def _dot(a, b):
    return jnp.dot(a, b, preferred_element_type=F32)


def _dot_nt(a, b):
    return lax.dot_general(a, b, (((1,), (1,)), ((), ())), preferred_element_type=F32)


def _dot_tn(a, b):
    return lax.dot_general(a, b, (((0,), (0,)), ((), ())), preferred_element_type=F32)


def _bdot(a, b):
    return lax.dot_general(a, b, (((2,), (1,)), ((0,), (0,))), preferred_element_type=F32)


def _bdot_nt(a, b):
    return lax.dot_general(a, b, (((2,), (2,)), ((0,), (0,))), preferred_element_type=F32)


def _bdot_tn(a, b):
    return lax.dot_general(a, b, (((1,), (1,)), ((0,), (0,))), preferred_element_type=F32)


def _tri(n, strict=False):
    r = lax.broadcasted_iota(I32, (n, n), 0)
    c = lax.broadcasted_iota(I32, (n, n), 1)
    return (r > c) if strict else (r >= c)


def _split_bf16(x):
    hi = x.astype(BF16)
    return hi, (x - hi.astype(F32)).astype(BF16)


def _pair_head_norm(x):
    lo = lax.broadcasted_iota(I32, x.shape, 1) < (LANES // 2)
    sq = x * x
    ms_lo = jnp.sum(jnp.where(lo, sq, 0.0), axis=-1, keepdims=True)
    ms_hi = jnp.sum(jnp.where(lo, 0.0, sq), axis=-1, keepdims=True)
    return x * lax.rsqrt(jnp.where(lo, ms_lo, ms_hi) * (2.0 / LANES) + EPS)


def _norm_matmul_body(x_ref, g_ref, w_ref, wgh_ref, wgl_ref, post_ref, o_ref, gate_ref, h_scr, *, norm_tiles):
    j = pl.program_id(1)

    @pl.when(j == 0)
    def _():
        hi, lo = _split_bf16(_rms_rows(x_ref[...]) * g_ref[...])
        h_scr[...] = hi
        gate_ref[...] = _dot(hi, wgh_ref[...]) + _dot(lo, wgh_ref[...]) + _dot(hi, wgl_ref[...])

    acc = _dot(h_scr[...], w_ref[...])
    if not norm_tiles:
        o_ref[...] = acc.astype(o_ref.dtype)
        return
    is_norm = functools.reduce(jnp.logical_or, [j == tile for tile in norm_tiles])

    @pl.when(is_norm)
    def _():
        for s in range(acc.shape[1] // LANES):
            cols = slice(s * LANES, (s + 1) * LANES)
            o_ref[:, cols] = (_pair_head_norm(acc[:, cols]) * post_ref[:, cols]).astype(o_ref.dtype)

    @pl.when(jnp.logical_not(is_norm))
    def _():
        o_ref[...] = acc.astype(o_ref.dtype)


def _norm_matmul(x, gain, w, w_gate, post=None, norm_tiles=(), tm=1024, tn=1024):
    t, d = x.shape
    n = w.shape[1]
    tm = min(tm, t)
    wg_hi, wg_lo = _split_bf16(w_gate)
    if post is None:
        post = jnp.ones((1, n), F32)
    return pl.pallas_call(
        functools.partial(_norm_matmul_body, norm_tiles=tuple(norm_tiles)),
        grid=(t // tm, n // tn),
        in_specs=[
            pl.BlockSpec((tm, d), lambda i, j: (i, 0)),
            pl.BlockSpec((1, d), lambda i, j: (0, 0)),
            pl.BlockSpec((d, tn), lambda i, j: (0, j)),
            pl.BlockSpec((d, LANES), lambda i, j: (0, 0)),
            pl.BlockSpec((d, LANES), lambda i, j: (0, 0)),
            pl.BlockSpec((1, tn), lambda i, j: (0, j)),
        ],
        out_specs=[
            pl.BlockSpec((tm, tn), lambda i, j: (i, j)),
            pl.BlockSpec((tm, LANES), lambda i, j: (i, 0)),
        ],
        out_shape=[jax.ShapeDtypeStruct((t, n), BF16), jax.ShapeDtypeStruct((t, LANES), F32)],
        scratch_shapes=[pltpu.VMEM((tm, d), BF16)],
        compiler_params=_cparams("parallel", "arbitrary"),
        name="norm_matmul",
    )(x, gain.reshape(1, d), w, wg_hi, wg_lo, post)


def _gates_body(pre_ref, aux_ref, col_ref, row_ref, *, lanes):
    pre = pre_ref[...] + aux_ref[0:1, :]
    scale = aux_ref[1:2, :]
    lane = lax.broadcasted_iota(I32, pre.shape, 1)
    sp = _softplus(pre)
    vals = jnp.zeros_like(pre)
    cum_mask = jnp.zeros(pre.shape, jnp.bool_)
    for lo, hi, kind, cum in lanes:
        m = (lane >= lo) & (lane < hi)
        if kind == "id":
            v = pre
        elif kind == "logsig":
            v = pre - sp
        elif kind == "softplus":
            v = scale * sp
        else:
            v = _sigmoid(pre)
        vals = jnp.where(m, v, vals)
        if cum:
            cum_mask = cum_mask | m
    pos = lax.broadcasted_iota(I32, pre.shape, 0) & (CHUNK - 1)
    c = vals
    k = 1
    while k < CHUNK:
        c = c + jnp.where(pos >= k, pltpu.roll(c, k, 0), 0.0)
        k *= 2
    out = jnp.where(cum_mask, c, vals)
    col_ref[...] = out
    row_ref[...] = out.T


def _gates(pre, aux, lanes, tm=2048):
    t = pre.shape[0]
    tm = min(tm, t)
    return pl.pallas_call(
        functools.partial(_gates_body, lanes=lanes),
        grid=(t // tm,),
        in_specs=[
            pl.BlockSpec((tm, LANES), lambda i: (i, 0)),
            pl.BlockSpec((8, LANES), lambda i: (0, 0)),
        ],
        out_specs=[
            pl.BlockSpec((tm, LANES), lambda i: (i, 0)),
            pl.BlockSpec((LANES, tm), lambda i: (0, i)),
        ],
        out_shape=[jax.ShapeDtypeStruct((t, LANES), F32), jax.ShapeDtypeStruct((LANES, t), F32)],
        compiler_params=_cparams("parallel"),
        name="gates",
    )(pre, aux)


def _mlstm_body(q_ref, k_ref, v_ref, og_ref, gcol_ref, grow_ref, mn_ref, o_ref, c_scr, n_scr, m_scr, *, ts):
    @pl.when(pl.program_id(1) == 0)
    def _():
        c_scr[...] = jnp.zeros_like(c_scr)
        n_scr[...] = jnp.zeros_like(n_scr)
        m_scr[...] = jnp.zeros_like(m_scr)

    scale = MLSTM_DQK ** -0.5
    nh, nc = MLSTM_HEADS, ts // CHUNK
    nb = nh * nc
    q = jnp.stack([q_ref[:, h * MLSTM_DQK:(h + 1) * MLSTM_DQK] for h in range(nh)], axis=0).reshape(nb, CHUNK, MLSTM_DQK)
    k = jnp.stack([k_ref[:, h * MLSTM_DQK:(h + 1) * MLSTM_DQK] for h in range(nh)], axis=0).reshape(nb, CHUNK, MLSTM_DQK)
    v = jnp.stack([v_ref[:, h * MLSTM_DV:(h + 1) * MLSTM_DV] for h in range(nh)], axis=0).reshape(nb, CHUNK, MLSTM_DV)
    gcol = gcol_ref[...]
    grow = grow_ref[...]
    li_col = jnp.stack([gcol[:, h:h + 1] for h in range(nh)], axis=0).reshape(nb, CHUNK, 1)
    b_col = jnp.stack([gcol[:, nh + h:nh + h + 1] for h in range(nh)], axis=0).reshape(nb, CHUNK, 1)
    li_row = jnp.stack([grow[0:nh, c * CHUNK:(c + 1) * CHUNK] for c in range(nc)], axis=1).reshape(nb, 1, CHUNK)
    b_row = jnp.stack([grow[nh:2 * nh, c * CHUNK:(c + 1) * CHUNK] for c in range(nc)], axis=1).reshape(nb, 1, CHUNK)

    b_end = b_col[:, CHUNK - 1:CHUNK, :]
    g_end = b_end - b_col + li_col
    m_loc = jnp.max(g_end, axis=1, keepdims=True)
    kw = k.astype(F32) * jnp.exp(g_end - m_loc)
    upd_c = _bdot_tn(kw.astype(BF16), v).reshape(nh, nc, MLSTM_DQK, MLSTM_DV)
    upd_n = jnp.sum(kw, axis=1, keepdims=True).reshape(nh, nc, 1, MLSTM_DQK)
    log_d = jnp.where(_tri(CHUNK)[None], b_col - b_row + li_row, -jnp.inf)
    max_d = jnp.max(log_d, axis=2, keepdims=True)
    s_raw = _bdot_nt(q, k)

    b_end4 = b_end.reshape(nh, nc, 1, 1)
    m_loc4 = m_loc.reshape(nh, nc, 1, 1)
    m_prev = m_scr[...][:, 0:1, 0:1]
    m_prevs, s_olds, s_upds = [], [], []
    for c in range(nc):
        m_prevs.append(m_prev)
        m_new = jnp.maximum(b_end4[:, c] + m_prev, m_loc4[:, c])
        s_olds.append(jnp.exp(b_end4[:, c] + m_prev - m_new))
        s_upds.append(jnp.exp(m_loc4[:, c] - m_new))
        m_prev = m_new
    m_scr[...] = jnp.broadcast_to(m_prev, m_scr.shape)

    inter_log = b_col + jnp.stack(m_prevs, axis=1).reshape(nb, 1, 1)
    m_t = jnp.maximum(inter_log, max_d)
    w_inter = jnp.exp(inter_log - m_t)
    scores = s_raw * (scale * jnp.exp(log_d - m_t))
    num = _bdot(scores.astype(BF16), v)
    den = jnp.sum(scores, axis=2, keepdims=True)

    q4 = q.reshape(nh, nc, CHUNK, MLSTM_DQK)
    c_state = c_scr[...]
    n_state = n_scr[...][:, 0:1, :]
    qcs, qns = [], []
    for c in range(nc):
        qcs.append(_bdot(q4[:, c], c_state.astype(BF16)))
        qns.append(jnp.sum(q4[:, c].astype(F32) * n_state, axis=2, keepdims=True))
        c_state = s_olds[c] * c_state + s_upds[c] * upd_c[:, c]
        n_state = s_olds[c] * n_state + s_upds[c] * upd_n[:, c]
    c_scr[...] = c_state
    n_scr[...] = jnp.broadcast_to(n_state, n_scr.shape)

    num = num + w_inter * (jnp.stack(qcs, axis=1).reshape(nb, CHUNK, MLSTM_DV) * scale)
    den = den + w_inter * (jnp.stack(qns, axis=1).reshape(nb, CHUNK, 1) * scale)
    hout = num / jnp.maximum(jnp.abs(den), jnp.exp(-m_t))
    hn = _rms_rows(hout).reshape(nh, ts, MLSTM_DV)
    for h in range(nh):
        cols = slice(h * MLSTM_DV, (h + 1) * MLSTM_DV)
        o_ref[:, cols] = (hn[h] * mn_ref[:, cols] * _sigmoid(og_ref[:, cols].astype(F32))).astype(o_ref.dtype)


def _mlstm(p, gcol, grow, m_norm, batch, seq, ts=256):
    t = batch * seq
    ts = min(ts, seq)
    nt = seq // ts
    qk_w = MLSTM_HEADS * MLSTM_DQK
    v_w = MLSTM_HEADS * MLSTM_DV
    row = lambda b, i: b * nt + i
    return pl.pallas_call(
        functools.partial(_mlstm_body, ts=ts),
        grid=(batch, nt),
        in_specs=[
            pl.BlockSpec((ts, qk_w), lambda b, i: (row(b, i), 0)),
            pl.BlockSpec((ts, qk_w), lambda b, i: (row(b, i), 1)),
            pl.BlockSpec((ts, v_w), lambda b, i: (row(b, i), 1)),
            pl.BlockSpec((ts, v_w), lambda b, i: (row(b, i), 2)),
            pl.BlockSpec((ts, LANES), lambda b, i: (row(b, i), 0)),
            pl.BlockSpec((8, ts), lambda b, i: (0, row(b, i))),
            pl.BlockSpec((1, v_w), lambda b, i: (0, 0)),
        ],
        out_specs=pl.BlockSpec((ts, v_w), lambda b, i: (row(b, i), 0)),
        out_shape=jax.ShapeDtypeStruct((t, v_w), BF16),
        scratch_shapes=[
            pltpu.VMEM((MLSTM_HEADS, MLSTM_DQK, MLSTM_DV), F32),
            pltpu.VMEM((MLSTM_HEADS, 8, MLSTM_DQK), F32),
            pltpu.VMEM((MLSTM_HEADS, 8, LANES), F32),
        ],
        compiler_params=_cparams("parallel", "arbitrary"),
        name="mlstm",
    )(p, p, p, p, gcol, grow, m_norm.reshape(1, v_w))


def _causal_conv(x_ref, tail_ref, w_ref, cols, ts, bias=None):
    xp = jnp.concatenate([tail_ref[:, cols], x_ref[:, cols].astype(F32)], axis=0)
    y = w_ref[CONV_WIDTH - 1:CONV_WIDTH, cols] * xp[8:, :]
    for j in range(CONV_WIDTH - 1):
        y = y + w_ref[j:j + 1, cols] * pltpu.roll(xp, CONV_WIDTH - 1 - j, 0)[8:, :]
    if bias is not None:
        y = y + bias[:, cols]
    return y


def _gdn_body(x_ref, z_ref, gcol_ref, grow_ref, cw_ref, gn_ref, o_ref, tail_scr, s_scr, *, ts):
    @pl.when(pl.program_id(1) == 0)
    def _():
        tail_scr[...] = jnp.zeros_like(tail_scr)
        s_scr[...] = jnp.zeros_like(s_scr)

    nh, nc = GDN_HEADS, ts // CHUNK
    nb = nh * nc
    qk_w = nh * GDN_DK

    def conv_heads(base):
        return jnp.stack([_silu(_causal_conv(x_ref, tail_scr, cw_ref, slice(base + h * GDN_DK, base + (h + 1) * GDN_DK), ts))
                          for h in range(nh)], axis=0)

    q = conv_heads(0)
    k = conv_heads(qk_w)
    v = conv_heads(2 * qk_w).reshape(nb, CHUNK, GDN_DV)
    q = (q * lax.rsqrt(jnp.sum(q * q, axis=-1, keepdims=True) + EPS) * (GDN_DK ** -0.5)).reshape(nb, CHUNK, GDN_DK)
    k = (k * lax.rsqrt(jnp.sum(k * k, axis=-1, keepdims=True) + EPS)).reshape(nb, CHUNK, GDN_DK)

    gcol = gcol_ref[...]
    grow = grow_ref[...]
    g_col = jnp.stack([gcol[:, 16 + h:17 + h] for h in range(nh)], axis=0).reshape(nb, CHUNK, 1)
    beta = jnp.stack([gcol[:, 24 + h:25 + h] for h in range(nh)], axis=0).reshape(nb, CHUNK, 1)
    g_row = jnp.stack([grow[:, c * CHUNK:(c + 1) * CHUNK] for c in range(nc)], axis=1).reshape(nb, 1, CHUNK)
    g_end = g_col[:, CHUNK - 1:CHUNK, :]

    tri = _tri(CHUNK)[None]
    tri_strict = _tri(CHUNK, strict=True)[None]
    eye = (lax.broadcasted_iota(I32, (CHUNK, CHUNK), 0) == lax.broadcasted_iota(I32, (CHUNK, CHUNK), 1)).astype(F32)[None]
    kb = k.astype(BF16)
    decay = jnp.exp(jnp.where(tri, g_col - g_row, -jnp.inf))
    qk_kk = _bdot_nt(jnp.concatenate([q, k], axis=1).astype(BF16), kb)
    a_mat = jnp.where(tri_strict, beta * qk_kk[:, CHUNK:] * decay, 0.0)
    p = -a_mat
    y = eye + p
    pb = p.astype(BF16)
    p = _bdot(pb, pb)
    for _ in range(4):
        py = _bdot(p.astype(BF16), jnp.concatenate([y, p], axis=2).astype(BF16))
        y = y + py[:, :, :CHUNK]
        p = py[:, :, CHUNK:]
    y = y + _bdot(p.astype(BF16), y.astype(BF16))
    eg = jnp.exp(g_col)
    rhs = jnp.concatenate([beta * v, (beta * eg) * k], axis=2).astype(BF16)
    sol = _bdot(y.astype(BF16), rhs)
    u = sol[:, :, :GDN_DV].reshape(nh, nc, CHUNK, GDN_DV)
    wq = jnp.concatenate([sol[:, :, GDN_DV:], q * eg], axis=1).astype(BF16).reshape(nh, nc, 2 * CHUNK, GDN_DK)
    qk = (qk_kk[:, :CHUNK] * decay).astype(BF16).reshape(nh, nc, CHUNK, CHUNK)
    k_dec = (k * jnp.exp(g_end - g_col)).astype(BF16).reshape(nh, nc, CHUNK, GDN_DK)
    g_tot = jnp.exp(g_end).reshape(nh, nc, 1, 1)

    s = s_scr[...]
    outs = []
    for c in range(nc):
        ws = _bdot(wq[:, c], s.astype(BF16))
        vb = (u[:, c] - ws[:, :CHUNK]).astype(BF16)
        outs.append(ws[:, CHUNK:] + _bdot(qk[:, c], vb))
        s = g_tot[:, c] * s + _bdot_tn(k_dec[:, c], vb)
    s_scr[...] = s

    o = jnp.stack(outs, axis=1).reshape(nh, ts, GDN_DV)
    on = _rms_rows(o) * gn_ref[...]
    for h in range(nh):
        cols = slice(h * GDN_DV, (h + 1) * GDN_DV)
        o_ref[:, cols] = (on[h] * _silu(z_ref[:, cols].astype(F32))).astype(o_ref.dtype)
    tail_scr[...] = x_ref[ts - 8:ts, :].astype(F32)


def _gdn(p, gcol, grow, conv_w, g_norm, batch, seq, ts=256):
    t = batch * seq
    ts = min(ts, seq)
    nt = seq // ts
    qkv_w = 3 * GDN_HEADS * GDN_DK
    v_w = GDN_HEADS * GDN_DV
    row = lambda b, i: b * nt + i
    return pl.pallas_call(
        functools.partial(_gdn_body, ts=ts),
        grid=(batch, nt),
        in_specs=[
            pl.BlockSpec((ts, qkv_w), lambda b, i: (row(b, i), 1)),
            pl.BlockSpec((ts, v_w), lambda b, i: (row(b, i), 6)),
            pl.BlockSpec((ts, LANES), lambda b, i: (row(b, i), 0)),
            pl.BlockSpec((8, ts), lambda b, i: (2, row(b, i))),
            pl.BlockSpec((CONV_WIDTH, qkv_w), lambda b, i: (0, 0)),
            pl.BlockSpec((1, GDN_DV), lambda b, i: (0, 0)),
        ],
        out_specs=pl.BlockSpec((ts, v_w), lambda b, i: (row(b, i), 0)),
        out_shape=jax.ShapeDtypeStruct((t, v_w), BF16),
        scratch_shapes=[
            pltpu.VMEM((8, qkv_w), F32),
            pltpu.VMEM((GDN_HEADS, GDN_DK, GDN_DV), F32),
        ],
        compiler_params=_cparams("parallel", "arbitrary"),
        name="gdn",
    )(p, p, gcol, grow, conv_w, g_norm.reshape(1, GDN_DV))


def _out_proj_body(x_ref, a_ref, b_ref, wa_ref, wb_ref, o_ref):
    o_ref[...] = x_ref[...] + _dot(a_ref[...], wa_ref[...]) + _dot(b_ref[...], wb_ref[...])


def _out_proj(x, a, b, wa, wb, tm=512):
    t, d = x.shape
    tm = min(tm, t)
    ka, kb = a.shape[1], b.shape[1]
    return pl.pallas_call(
        _out_proj_body,
        grid=(t // tm,),
        in_specs=[
            pl.BlockSpec((tm, d), lambda i: (i, 0)),
            pl.BlockSpec((tm, ka), lambda i: (i, 0)),
            pl.BlockSpec((tm, kb), lambda i: (i, 0)),
            pl.BlockSpec((ka, d), lambda i: (0, 0)),
            pl.BlockSpec((kb, d), lambda i: (0, 0)),
        ],
        out_specs=pl.BlockSpec((tm, d), lambda i: (i, 0)),
        out_shape=jax.ShapeDtypeStruct((t, d), F32),
        compiler_params=_cparams("parallel"),
        name="out_proj",
    )(x, a, b, wa, wb)


def _lane_table(entries):
    row = jnp.zeros((LANES,), F32)
    for start, vals in entries:
        row = lax.dynamic_update_slice(row, vals.astype(F32).reshape(-1), (start,))
    return row.reshape(1, LANES)


def _gate_weight(w_in, entries):
    w = jnp.zeros((w_in.shape[0], LANES), F32)
    for start, lo, hi in entries:
        w = lax.dynamic_update_slice(w, w_in[:, lo:hi].astype(F32), (0, start))
    return w


EVEN_GATE_LANES = ((0, 4, "id", False), (4, 8, "logsig", True), (16, 24, "softplus", True), (24, 32, "sigmoid", False))


def _even_mix(xf, batch, seq, gain, w_in, i_bias, f_bias, m_norm, conv_w, a_log, dt_bias, g_norm, w_out):
    w_big = jnp.concatenate([w_in[:, 0:3072], w_in[:, 3080:7176]], axis=1).astype(BF16)
    w_gate = _gate_weight(w_in, ((0, 3072, 3076), (4, 3076, 3080), (16, 7176, 7184), (24, 7184, 7192)))
    aux = jnp.concatenate([
        _lane_table(((0, i_bias), (4, f_bias), (16, dt_bias))),
        _lane_table(((16, -jnp.exp(a_log)),)),
        jnp.zeros((6, LANES), F32)], axis=0)
    p, pre = _norm_matmul(xf, gain, w_big, w_gate, tn=w_big.shape[1] // 4)
    gcol, grow = _gates(pre, aux, EVEN_GATE_LANES)
    hm = _mlstm(p, gcol, grow, m_norm, batch, seq)
    o = _gdn(p, gcol, grow, conv_w, g_norm, batch, seq)
    v_w = MLSTM_HEADS * MLSTM_DV
    return _out_proj(xf, hm, o, w_out[:v_w].astype(BF16), w_out[v_w:].astype(BF16))


ATT_TQ = 256


def _attn_body(q_ref, k0_ref, k1_ref, k2_ref, v0_ref, v1_ref, v2_ref, bias_ref, o_ref):
    i = pl.program_id(1)
    tq = q_ref.shape[0]
    lo_half = lax.broadcasted_iota(I32, (1, LANES), 1) < ATT_DH
    lo_rows = lax.broadcasted_iota(I32, (LANES, 1), 0) < ATT_DH
    k_refs = (k0_ref, k1_ref, k2_ref)
    v_refs = (v0_ref, v1_ref, v2_ref)
    for hp in range(ATT_HEADS // 2):
        cols = slice(hp * LANES, (hp + 1) * LANES)
        q = q_ref[:, cols]
        ks = [r[:, cols] for r in k_refs]
        vs = [r[:, cols] for r in v_refs]
        outs = []
        for hh in range(2):
            own = lo_half if hh == 0 else jnp.logical_not(lo_half)
            qm = jnp.where(own, q, jnp.zeros_like(q))
            s = [_dot_nt(ks[j], qm) + bias_ref[2 * hp + hh, j * tq:(j + 1) * tq, :] for j in range(3)]
            s[0] = jnp.where(i >= 2, s[0], -jnp.inf)
            s[1] = jnp.where(i >= 1, s[1], -jnp.inf)
            m = functools.reduce(jnp.maximum, [jnp.max(sj, axis=0, keepdims=True) for sj in s])
            o = functools.reduce(jnp.add, [_dot_tn(jnp.where(own, vs[j], jnp.ones_like(vs[j])),
                                                   jnp.exp((s[j] - m).astype(BF16))) for j in range(3)])
            outs.append(o / pltpu.roll(o, ATT_DH, 0))
        o_ref[:, cols] = jnp.where(lo_rows, outs[0], outs[1]).T.astype(o_ref.dtype)


def _attn(p, bias, batch, seq, col0):
    t = batch * seq
    tq = ATT_TQ
    nt = seq // tq
    w = ATT_HEADS * ATT_DH
    row = lambda b, i, back: b * nt + jnp.maximum(i - back, 0)
    kv_spec = lambda off, back: pl.BlockSpec((tq, w), lambda b, i: (row(b, i, back), col0 + off))
    return pl.pallas_call(
        _attn_body,
        grid=(batch, nt),
        in_specs=[
            pl.BlockSpec((tq, w), lambda b, i: (row(b, i, 0), col0)),
            kv_spec(1, 2), kv_spec(1, 1), kv_spec(1, 0),
            kv_spec(2, 2), kv_spec(2, 1), kv_spec(2, 0),
            pl.BlockSpec((ATT_HEADS, 3 * tq, tq), lambda b, i: (0, 0, 0)),
        ],
        out_specs=pl.BlockSpec((tq, w), lambda b, i: (row(b, i, 0), 0)),
        out_shape=jax.ShapeDtypeStruct((t, w), BF16),
        compiler_params=_cparams("parallel", "arbitrary"),
        name="band_attn",
    )(p, p, p, p, p, p, p, bias)


def _attn_bias_table(rel_bias):
    pad = ATT_LEFT_CHUNKS * CHUNK
    band = pad + CHUNK
    rel = jnp.arange(CHUNK)[:, None] + pad - jnp.arange(band)[None, :]
    rel_idx = jnp.clip(rel, -(CHUNK - 1), REL_MAX_PAST) + (CHUNK - 1)
    onehot = (rel_idx[None, :, :] == jnp.arange(rel_bias.shape[1])[:, None, None]).astype(F32)
    bias64 = jnp.einsum("hr,rlj->hlj", rel_bias.astype(F32), onehot, precision=HIGHEST)
    nq = ATT_TQ // CHUNK
    rows = [jnp.pad(bias64, ((0, 0), (0, 0), (a * CHUNK, (nq - 1 - a) * CHUNK)), constant_values=-jnp.inf)
            for a in range(nq)]
    return jnp.swapaxes(jnp.concatenate(rows, axis=1), 1, 2)


def _ssd_body(x_ref, z_ref, gcol_ref, grow_ref, cw_ref, cb_ref, dsk_ref, sn_ref, o_ref,
              tail_scr, xc_scr, y_scr, s_scr, *, ts):
    @pl.when(pl.program_id(1) == 0)
    def _():
        tail_scr[...] = jnp.zeros_like(tail_scr)
        s_scr[...] = jnp.zeros_like(s_scr)

    conv_w = x_ref.shape[1]
    for j in range(conv_w // LANES):
        cols = slice(j * LANES, (j + 1) * LANES)
        xc_scr[:, cols] = _silu(_causal_conv(x_ref, tail_scr, cw_ref, cols, ts, bias=cb_ref))

    nc, ng, npair = ts // CHUNK, SSD_GROUPS, SSD_HEADS // 2
    ppg = npair // ng
    nb = npair * nc
    gw = SSD_HPG * SSD_DH
    lo_half = lax.broadcasted_iota(I32, (1, LANES), 1) < SSD_DH
    gcol = gcol_ref[...]
    grow = grow_ref[...]

    def pair_cols(base):
        return jnp.stack([jnp.where(lo_half, gcol[:, base + p:base + p + 1], gcol[:, base + npair + p:base + npair + p + 1])
                          for p in range(npair)], axis=0).reshape(nb, CHUNK, LANES)

    dt = pair_cols(0)
    a_col = pair_cols(SSD_HEADS)
    a_row = jnp.stack([jnp.concatenate([grow[0:npair, c * CHUNK:(c + 1) * CHUNK], grow[npair:, c * CHUNK:(c + 1) * CHUNK]], axis=1)
                       for c in range(nc)], axis=1).reshape(nb, 1, LANES)
    a_end = a_col[:, CHUNK - 1:CHUNK, :]
    xs = jnp.stack([xc_scr[:, p * LANES:(p + 1) * LANES] for p in range(npair)], axis=0).reshape(nb, CHUNK, LANES)
    dsk = jnp.stack([dsk_ref[:, p * LANES:(p + 1) * LANES] for p in range(npair)], axis=0)
    dsk = jnp.broadcast_to(dsk[:, None], (npair, nc, 1, LANES)).reshape(nb, 1, LANES)
    bm = jnp.stack([xc_scr[:, SSD_INNER + g * SSD_STATE:SSD_INNER + (g + 1) * SSD_STATE] for g in range(ng)], axis=0)
    cm = jnp.stack([xc_scr[:, SSD_INNER + SSD_BC_W + g * SSD_STATE:SSD_INNER + SSD_BC_W + (g + 1) * SSD_STATE]
                    for g in range(ng)], axis=0)
    bm = bm.astype(BF16).reshape(ng * nc, CHUNK, SSD_STATE)
    cm = cm.astype(BF16).reshape(ng * nc, CHUNK, SSD_STATE)

    tri2 = (lax.broadcasted_iota(I32, (CHUNK, LANES), 0) >= (lax.broadcasted_iota(I32, (CHUNK, LANES), 1) & (SSD_DH - 1)))[None]
    decay = jnp.exp(jnp.where(tri2, a_col - a_row, -jnp.inf))
    cb = _bdot_nt(cm, bm)
    cb2 = jnp.concatenate([cb, cb], axis=2).reshape(ng, 1, nc, CHUNK, LANES)
    cb2 = jnp.broadcast_to(cb2, (ng, ppg, nc, CHUNK, LANES)).reshape(nb, CHUNK, LANES)
    xd = xs * dt
    rhs = jnp.concatenate([jnp.where(lo_half, xd, 0.0), jnp.where(lo_half, 0.0, xd)], axis=1).astype(BF16)
    y_in = (_bdot((cb2 * decay).astype(BF16), rhs) + dsk * xs).reshape(ng, ppg, nc, CHUNK, LANES)
    ea = jnp.exp(a_col).reshape(ng, ppg, nc, CHUNK, LANES)
    dxd = (xd * jnp.exp(a_end - a_col)).astype(BF16).reshape(ng, ppg, nc, CHUNK, LANES)
    cdec = jnp.exp(a_end).reshape(ng, ppg, nc, 1, LANES)
    bm = bm.reshape(ng, nc, CHUNK, SSD_STATE)
    cm = cm.reshape(ng, nc, CHUNK, SSD_STATE)

    s = s_scr[...]
    for c in range(nc):
        rows = slice(c * CHUNK, (c + 1) * CHUNK)
        y_off = _bdot(cm[:, c], s.astype(BF16))
        for j in range(ppg):
            yj = y_in[:, j, c] + y_off[:, :, j * LANES:(j + 1) * LANES] * ea[:, j, c]
            for g in range(ng):
                y_scr[rows, (g * ppg + j) * LANES:(g * ppg + j + 1) * LANES] = yj[g]
        dx = jnp.concatenate([dxd[:, j, c] for j in range(ppg)], axis=2)
        cd = jnp.concatenate([cdec[:, j, c] for j in range(ppg)], axis=2)
        s = s * cd + _bdot_tn(bm[:, c], dx)
    s_scr[...] = s

    for g in range(SSD_GROUPS):
        cols = slice(g * gw, (g + 1) * gw)
        yg = y_scr[:, cols] * _silu(z_ref[:, cols].astype(F32))
        o_ref[:, cols] = (_rms_rows(yg) * sn_ref[:, cols]).astype(o_ref.dtype)
    tail_scr[...] = x_ref[ts - 8:ts, :].astype(F32)


def _ssd(p, gcol, grow, conv_w, conv_b, d_skip, s_norm, batch, seq, ts=256):
    t = batch * seq
    ts = min(ts, seq)
    nt = seq // ts
    cw = SSD_INNER + 2 * SSD_BC_W
    row = lambda b, i: b * nt + i
    return pl.pallas_call(
        functools.partial(_ssd_body, ts=ts),
        grid=(batch, nt),
        in_specs=[
            pl.BlockSpec((ts, cw), lambda b, i: (row(b, i), 0)),
            pl.BlockSpec((ts, SSD_INNER), lambda b, i: (row(b, i), 3)),
            pl.BlockSpec((ts, LANES), lambda b, i: (row(b, i), 0)),
            pl.BlockSpec((SSD_HEADS, ts), lambda b, i: (1, row(b, i))),
            pl.BlockSpec((CONV_WIDTH, cw), lambda b, i: (0, 0)),
            pl.BlockSpec((1, cw), lambda b, i: (0, 0)),
            pl.BlockSpec((1, SSD_INNER), lambda b, i: (0, 0)),
            pl.BlockSpec((1, SSD_INNER), lambda b, i: (0, 0)),
        ],
        out_specs=pl.BlockSpec((ts, SSD_INNER), lambda b, i: (row(b, i), 0)),
        out_shape=jax.ShapeDtypeStruct((t, SSD_INNER), BF16),
        scratch_shapes=[
            pltpu.VMEM((8, cw), F32),
            pltpu.VMEM((ts, cw), F32),
            pltpu.VMEM((ts, SSD_INNER), F32),
            pltpu.VMEM((SSD_GROUPS, SSD_STATE, SSD_HPG * SSD_DH), F32),
        ],
        compiler_params=_cparams("parallel", "arbitrary"),
        name="ssd",
    )(p, p, gcol, grow, conv_w, conv_b.reshape(1, cw), jnp.repeat(d_skip, SSD_DH).reshape(1, SSD_INNER),
      s_norm.reshape(1, SSD_INNER))


ODD_GATE_LANES = ((0, 32, "softplus", False), (32, 64, "softplus", True))


def _odd_mix(xf, batch, seq, gain, w_in, q_gain, k_gain, rel_bias, conv_w, conv_b, a_log, dt_bias, d_skip, s_norm, w_out):
    w_big = jnp.concatenate([w_in[:, 5120:8192], w_in[:, 0:3072], w_in[:, 3072:5120]], axis=1).astype(BF16)
    even_odd = lambda a: jnp.concatenate([a[..., 0::2], a[..., 1::2]], axis=-1)
    w_dt = even_odd(w_in[:, 8192:8224])
    w_gate = _gate_weight(jnp.concatenate([w_dt, w_dt], axis=1), ((0, 0, 2 * SSD_HEADS),))
    aux = jnp.concatenate([
        _lane_table(((0, even_odd(dt_bias)), (SSD_HEADS, even_odd(dt_bias)))),
        _lane_table(((0, jnp.ones((SSD_HEADS,), F32)), (SSD_HEADS, -jnp.exp(even_odd(a_log))))),
        jnp.zeros((6, LANES), F32)], axis=0)
    a_w = ATT_HEADS * ATT_DH
    post = jnp.ones((1, w_big.shape[1]), F32)
    post = lax.dynamic_update_slice(post, jnp.tile(q_gain.astype(F32) * ATT_DH ** -0.5, ATT_HEADS).reshape(1, a_w), (0, 3072))
    post = lax.dynamic_update_slice(post, jnp.tile(k_gain.astype(F32), ATT_HEADS).reshape(1, a_w), (0, 3072 + a_w))
    p, pre = _norm_matmul(xf, gain, w_big, w_gate, post=post, norm_tiles=(3, 4))
    gcol, grow = _gates(pre, aux, ODD_GATE_LANES)
    attn = _attn(p, _attn_bias_table(rel_bias), batch, seq, col0=3072 // a_w)
    y = _ssd(p, gcol, grow, conv_w, conv_b, d_skip, s_norm, batch, seq)
    return _out_proj(xf, attn, y, w_out[:a_w].astype(BF16), w_out[a_w:].astype(BF16))


ROW_W = D_MODEL + LANES
MOE_TM = 512
DMA_ROWS = 512


def _router_body(x_ref, g_ref, wh_ref, wl_ref, b_ref, rows_ref, info_ref, cnt_ref, carry_scr):
    @pl.when(pl.program_id(0) == 0)
    def _():
        carry_scr[...] = jnp.zeros_like(carry_scr)

    h = _rms_rows(x_ref[...]) * g_ref[...]
    h_hi, h_lo = _split_bf16(h)
    logits = _dot(h_hi, wh_ref[...]) + _dot(h_lo, wh_ref[...]) + _dot(h_hi, wl_ref[...]) + b_ref[...]
    tm = logits.shape[0]
    lane = lax.broadcasted_iota(I32, logits.shape, 1).astype(F32)
    ninf = -jnp.inf

    def first_max(v):
        m = jnp.max(v, axis=-1, keepdims=True)
        return m, jnp.min(jnp.where(v == m, lane, float(LANES)), axis=-1, keepdims=True)

    glog = jnp.where(lane < MOE_GROUPS, logits, ninf)
    gmax, gidx = first_max(glog)
    grp_w = 1.0 / jnp.sum(jnp.exp(glog - gmax), axis=-1, keepdims=True)
    e_lo = MOE_GROUPS + gidx * MOE_EPG
    in_grp = (lane >= e_lo) & (lane < e_lo + MOE_EPG)
    elog = jnp.where(in_grp, logits, ninf)
    v1, i1 = first_max(elog)
    v2, i2 = first_max(jnp.where(lane == i1, ninf, elog))
    e21 = jnp.exp(v2 - v1)
    w1 = 1.0 / (1.0 + e21)
    comb = grp_w * (jnp.where(lane == i1, w1, 0.0) + jnp.where(lane == i2, e21 * w1, 0.0))
    local = jnp.zeros_like(comb)
    for g in range(MOE_GROUPS):
        local = local + pltpu.roll(comb, LANES - MOE_GROUPS - g * MOE_EPG, 1)
    local = jnp.where(lane < MOE_EPG, local, 0.0)
    rows_ref[...] = jnp.concatenate([h, local], axis=1)

    onehot = jnp.where(lane == gidx, 1.0, 0.0)
    lower = jnp.where(_tri(tm, strict=True), 1.0, 0.0).astype(BF16)
    before = _dot(lower, onehot.astype(BF16)) + carry_scr[0:1, :]
    rank = jnp.sum(onehot * before, axis=-1, keepdims=True)
    carry = carry_scr[0:1, :] + jnp.sum(onehot, axis=0, keepdims=True)
    carry_scr[...] = jnp.broadcast_to(carry, carry_scr.shape)
    cnt_ref[...] = jnp.broadcast_to(carry, cnt_ref.shape)
    info = jnp.where(lane == 0, gidx, jnp.where(lane == 1, rank, 0.0))
    info_ref[...] = info.T[0:8, :]


def _router(x, gain, w, b, tm=256):
    t, d = x.shape
    tm = min(tm, t)
    w_hi, w_lo = _split_bf16(w)
    return pl.pallas_call(
        _router_body,
        grid=(t // tm,),
        in_specs=[
            pl.BlockSpec((tm, d), lambda i: (i, 0)),
            pl.BlockSpec((1, d), lambda i: (0, 0)),
            pl.BlockSpec((d, LANES), lambda i: (0, 0)),
            pl.BlockSpec((d, LANES), lambda i: (0, 0)),
            pl.BlockSpec((1, LANES), lambda i: (0, 0)),
        ],
        out_specs=[
            pl.BlockSpec((tm, ROW_W), lambda i: (i, 0)),
            pl.BlockSpec((8, tm), lambda i: (0, i)),
            pl.BlockSpec((8, LANES), lambda i: (0, 0)),
        ],
        out_shape=[
            jax.ShapeDtypeStruct((t, ROW_W), F32),
            jax.ShapeDtypeStruct((8, t), F32),
            jax.ShapeDtypeStruct((8, LANES), F32),
        ],
        scratch_shapes=[pltpu.VMEM((8, LANES), F32)],
        compiler_params=_cparams("arbitrary"),
        name="moe_router",
    )(x, gain.reshape(1, d), w_hi, w_lo, b)


def _dispatch_body(pos_ref, rows_ref, init_ref, out_ref, sem):
    del init_ref
    base = pl.program_id(0) * DMA_ROWS

    def row_copy(r):
        return pltpu.make_async_copy(rows_ref.at[pl.ds(r, 1)], out_ref.at[pl.ds(pos_ref[base + r], 1)], sem)

    def start(r, carry):
        row_copy(r).start()
        return carry

    lax.fori_loop(0, DMA_ROWS, start, 0, unroll=8)
    pltpu.make_async_copy(rows_ref, out_ref.at[pl.ds(0, DMA_ROWS)], sem).wait()


def _dispatch(pos, rows, n_sorted):
    t, w = rows.shape
    return pl.pallas_call(
        _dispatch_body,
        grid_spec=pltpu.PrefetchScalarGridSpec(
            num_scalar_prefetch=1,
            grid=(t // DMA_ROWS,),
            in_specs=[
                pl.BlockSpec((DMA_ROWS, w), lambda i, pos: (i, 0)),
                pl.BlockSpec(memory_space=pl.ANY),
            ],
            out_specs=pl.BlockSpec(memory_space=pl.ANY),
            scratch_shapes=[pltpu.SemaphoreType.DMA(())],
        ),
        out_shape=jax.ShapeDtypeStruct((n_sorted, w), rows.dtype),
        input_output_aliases={2: 0},
        compiler_params=_cparams("arbitrary"),
        name="moe_dispatch",
    )(pos, rows, jnp.zeros((n_sorted, w), rows.dtype))


def _experts_body(tile_group_ref, n_used_ref, xs_ref, wg_ref, wu_ref, wd_ref, y_ref, act_scr):
    del tile_group_ref
    i = pl.program_id(0)

    @pl.when(i < n_used_ref[0])
    def _():
        x = xs_ref[:, :D_MODEL].astype(BF16)
        comb = xs_ref[:, D_MODEL:]
        for e in range(MOE_EPG):
            a = _dot(x, wg_ref[e])
            u = _dot(x, wu_ref[e])
            act_scr[:, e * D_EXPERT:(e + 1) * D_EXPERT] = (_silu(a) * u * comb[:, e:e + 1]).astype(BF16)
        y_ref[...] = _dot(act_scr[...], wd_ref[...].reshape(MOE_EPG * D_EXPERT, D_MODEL))

    @pl.when(i >= n_used_ref[0])
    def _():
        y_ref[...] = jnp.zeros_like(y_ref)


def _experts(tile_group, n_used, xs, w_gate, w_up, w_down):
    n_sorted = xs.shape[0]
    tm = MOE_TM
    w_in_spec = pl.BlockSpec((None, MOE_EPG, D_MODEL, D_EXPERT), lambda i, tg, nu: (tg[i], 0, 0, 0))
    return pl.pallas_call(
        _experts_body,
        grid_spec=pltpu.PrefetchScalarGridSpec(
            num_scalar_prefetch=2,
            grid=(n_sorted // tm,),
            in_specs=[
                pl.BlockSpec((tm, ROW_W), lambda i, tg, nu: (i, 0)),
                w_in_spec,
                w_in_spec,
                pl.BlockSpec((None, MOE_EPG, D_EXPERT, D_MODEL), lambda i, tg, nu: (tg[i], 0, 0, 0)),
            ],
            out_specs=pl.BlockSpec((tm, D_MODEL), lambda i, tg, nu: (i, 0)),
            scratch_shapes=[pltpu.VMEM((tm, MOE_EPG * D_EXPERT), BF16)],
        ),
        out_shape=jax.ShapeDtypeStruct((n_sorted, D_MODEL), F32),
        compiler_params=_cparams("arbitrary"),
        name="moe_experts",
    )(tile_group, n_used, xs, w_gate, w_up, w_down)


def _combine_body(pos_ref, x_ref, y_ref, o_ref, buf, sem):
    base = pl.program_id(0) * DMA_ROWS

    def row_copy(r):
        return pltpu.make_async_copy(y_ref.at[pl.ds(pos_ref[base + r], 1)], buf.at[pl.ds(r, 1)], sem)

    def start(r, carry):
        row_copy(r).start()
        return carry

    lax.fori_loop(0, DMA_ROWS, start, 0, unroll=8)
    pltpu.make_async_copy(y_ref.at[pl.ds(0, DMA_ROWS)], buf, sem).wait()
    o_ref[...] = x_ref[...] + buf[...]


def _combine(pos, x, y):
    t, d = x.shape
    return pl.pallas_call(
        _combine_body,
        grid_spec=pltpu.PrefetchScalarGridSpec(
            num_scalar_prefetch=1,
            grid=(t // DMA_ROWS,),
            in_specs=[
                pl.BlockSpec((DMA_ROWS, d), lambda i, pos: (i, 0)),
                pl.BlockSpec(memory_space=pl.ANY),
            ],
            out_specs=pl.BlockSpec((DMA_ROWS, d), lambda i, pos: (i, 0)),
            scratch_shapes=[pltpu.VMEM((DMA_ROWS, d), F32), pltpu.SemaphoreType.DMA(())],
        ),
        out_shape=jax.ShapeDtypeStruct((t, d), F32),
        compiler_params=_cparams("arbitrary"),
        name="moe_combine",
    )(pos, x, y)


def _moe(xf, gain, wg, bg, we, be, w_gate, w_up, w_down):
    t = xf.shape[0]
    w_r = jnp.zeros((D_MODEL, LANES), F32)
    w_r = lax.dynamic_update_slice(w_r, wg.astype(F32), (0, 0))
    w_r = lax.dynamic_update_slice(w_r, we.astype(F32), (0, MOE_GROUPS))
    b_r = _lane_table(((0, bg), (MOE_GROUPS, be)))
    rows, info, cnt = _router(xf, gain, w_r, b_r)

    gidx = info[0].astype(I32)
    rank = info[1].astype(I32)
    counts = cnt[0, :MOE_GROUPS].astype(I32)
    padded = ((counts + MOE_TM - 1) // MOE_TM) * MOE_TM
    ends = jnp.cumsum(padded)
    starts = ends - padded
    pos = rank + sum(jnp.where(gidx == g, starts[g], 0) for g in range(MOE_GROUPS))
    n_sorted = t + MOE_GROUPS * MOE_TM
    n_tiles = n_sorted // MOE_TM
    tile_start = jnp.arange(n_tiles, dtype=I32) * MOE_TM
    tile_group = jnp.minimum(jnp.sum(tile_start[:, None] >= ends[None, :], axis=1), MOE_GROUPS - 1).astype(I32)
    n_used = (ends[-1:] // MOE_TM).astype(I32)

    xs = _dispatch(pos, rows, n_sorted)
    ys = _experts(tile_group, n_used, xs, w_gate.astype(BF16), w_up.astype(BF16), w_down.astype(BF16))
    return _combine(pos, xf, ys)


def kernel(x, norm_mix, norm_ffn, even_w_in, mlstm_i_bias, mlstm_f_bias, mlstm_norm, gdn_conv, gdn_A_log, gdn_dt_bias,
           gdn_norm, even_w_out, odd_w_in, att_q_norm, att_k_norm, att_rel_bias, ssd_conv, ssd_conv_bias, ssd_A_log,
           ssd_dt_bias, ssd_D, ssd_norm, odd_w_out, router_group_w, router_group_b, router_expert_w, router_expert_b,
           moe_w_gate, moe_w_up, moe_w_down):
    batch, seq, d = x.shape
    xf = x.reshape(batch * seq, d)
    depth = norm_mix.shape[0]
    for layer in range(depth):
        i = layer // 2
        if layer % 2 == 0:
            xf = _even_mix(xf, batch, seq, norm_mix[layer], even_w_in[i], mlstm_i_bias[i], mlstm_f_bias[i], mlstm_norm[i],
                           gdn_conv[i], gdn_A_log[i], gdn_dt_bias[i], gdn_norm[i], even_w_out[i])
        else:
            xf = _odd_mix(xf, batch, seq, norm_mix[layer], odd_w_in[i], att_q_norm[i], att_k_norm[i], att_rel_bias[i],
                          ssd_conv[i], ssd_conv_bias[i], ssd_A_log[i], ssd_dt_bias[i], ssd_D[i], ssd_norm[i], odd_w_out[i])
        xf = _moe(xf, norm_ffn[layer], router_group_w[layer], router_group_b[layer], router_expert_w[layer],
                  router_expert_b[layer], moe_w_gate[layer], moe_w_up[layer], moe_w_down[layer])
    return xf.reshape(batch, seq, d)
```

```python
import functools

import jax
import jax.numpy as jnp
from jax import lax
from jax.experimental import pallas as pl
from jax.experimental.pallas import tpu as pltpu

F32 = jnp.float32
BF16 = jnp.bfloat16
I32 = jnp.int32
HIGHEST = lax.Precision.HIGHEST

D_MODEL = 1024
CHUNK = 64
CONV_WIDTH = 4
EPS = 1e-6
LANES = 128

MLSTM_HEADS, MLSTM_DQK, MLSTM_DV = 4, 128, 256
GDN_HEADS, GDN_DK, GDN_DV = 8, 128, 128
ATT_HEADS, ATT_DH, ATT_LEFT_CHUNKS, REL_MAX_PAST = 16, 64, 8, 256
SSD_HEADS, SSD_DH, SSD_STATE, SSD_GROUPS = 32, 64, 128, 4
SSD_HPG = SSD_HEADS // SSD_GROUPS
SSD_INNER = SSD_HEADS * SSD_DH
SSD_BC_W = SSD_GROUPS * SSD_STATE
MOE_GROUPS, MOE_EPG, D_EXPERT = 4, 8, 256

VMEM_LIMIT = 48 * 1024 * 1024


def _cparams(*sem):
    return pltpu.CompilerParams(dimension_semantics=sem, vmem_limit_bytes=VMEM_LIMIT)


def _softplus(x):
    return jnp.maximum(x, 0.0) + jnp.log1p(jnp.exp(-jnp.abs(x)))


def _sigmoid(x):
    return 1.0 / (1.0 + jnp.exp(-x))


def _silu(x):
    return x * _sigmoid(x)


def _rms_rows(x):
    return x * lax.rsqrt(jnp.mean(x * x, axis=-1, keepdims=True) + EPS)


def _dot(a, b):
    return jnp.dot(a, b, preferred_element_type=F32)


def _dot_nt(a, b):
    return lax.dot_general(a, b, (((1,), (1,)), ((), ())), preferred_element_type=F32)


def _dot_tn(a, b):
    return lax.dot_general(a, b, (((0,), (0,)), ((), ())), preferred_element_type=F32)


def _bdot(a, b):
    return lax.dot_general(a, b, (((2,), (1,)), ((0,), (0,))), preferred_element_type=F32)


def _bdot_nt(a, b):
    return lax.dot_general(a, b, (((2,), (2,)), ((0,), (0,))), preferred_element_type=F32)


def _bdot_tn(a, b):
    return lax.dot_general(a, b, (((1,), (1,)), ((0,), (0,))), preferred_element_type=F32)


def _tri(n, strict=False):
    r = lax.broadcasted_iota(I32, (n, n), 0)
    c = lax.broadcasted_iota(I32, (n, n), 1)
    return (r > c) if strict else (r >= c)


def _split_bf16(x):
    hi = x.astype(BF16)
    return hi, (x - hi.astype(F32)).astype(BF16)


def _pair_head_norm(x):
    lo = lax.broadcasted_iota(I32, x.shape, 1) < (LANES // 2)
    sq = x * x
    ms_lo = jnp.sum(jnp.where(lo, sq, 0.0), axis=-1, keepdims=True)
    ms_hi = jnp.sum(jnp.where(lo, 0.0, sq), axis=-1, keepdims=True)
    return x * lax.rsqrt(jnp.where(lo, ms_lo, ms_hi) * (2.0 / LANES) + EPS)


def _norm_matmul_body(x_ref, g_ref, w_ref, wgh_ref, wgl_ref, post_ref, o_ref, gate_ref, h_scr, *, norm_tiles):
    j = pl.program_id(1)

    @pl.when(j == 0)
    def _():
        hi, lo = _split_bf16(_rms_rows(x_ref[...]) * g_ref[...])
        h_scr[...] = hi
        gate_ref[...] = _dot(hi, wgh_ref[...]) + _dot(lo, wgh_ref[...]) + _dot(hi, wgl_ref[...])

    acc = _dot(h_scr[...], w_ref[...])
    if not norm_tiles:
        o_ref[...] = acc.astype(o_ref.dtype)
        return
    is_norm = functools.reduce(jnp.logical_or, [j == tile for tile in norm_tiles])

    @pl.when(is_norm)
    def _():
        half = LANES // 2
        same_head = (lax.broadcasted_iota(I32, (LANES, LANES), 0) >= half) == (lax.broadcasted_iota(I32, (LANES, LANES), 1) >= half)
        avg = jnp.where(same_head, 1.0 / half, 0.0).astype(BF16)
        for s in range(acc.shape[1] // LANES):
            cols = slice(s * LANES, (s + 1) * LANES)
            blk = acc[:, cols]
            ms = _dot((blk * blk).astype(BF16), avg)
            o_ref[:, cols] = (blk * lax.rsqrt(ms + EPS) * post_ref[:, cols]).astype(o_ref.dtype)

    @pl.when(jnp.logical_not(is_norm))
    def _():
        o_ref[...] = acc.astype(o_ref.dtype)


def _norm_matmul(x, gain, w, w_gate, post=None, norm_tiles=(), tm=1024, tn=1024):
    t, d = x.shape
    n = w.shape[1]
    tm = min(tm, t)
    wg_hi, wg_lo = _split_bf16(w_gate)
    if post is None:
        post = jnp.ones((1, n), F32)
    return pl.pallas_call(
        functools.partial(_norm_matmul_body, norm_tiles=tuple(norm_tiles)),
        grid=(t // tm, n // tn),
        in_specs=[
            pl.BlockSpec((tm, d), lambda i, j: (i, 0)),
            pl.BlockSpec((1, d), lambda i, j: (0, 0)),
            pl.BlockSpec((d, tn), lambda i, j: (0, j)),
            pl.BlockSpec((d, LANES), lambda i, j: (0, 0)),
            pl.BlockSpec((d, LANES), lambda i, j: (0, 0)),
            pl.BlockSpec((1, tn), lambda i, j: (0, j)),
        ],
        out_specs=[
            pl.BlockSpec((tm, tn), lambda i, j: (i, j)),
            pl.BlockSpec((tm, LANES), lambda i, j: (i, 0)),
        ],
        out_shape=[jax.ShapeDtypeStruct((t, n), BF16), jax.ShapeDtypeStruct((t, LANES), F32)],
        scratch_shapes=[pltpu.VMEM((tm, d), BF16)],
        compiler_params=_cparams("parallel", "arbitrary"),
        name="norm_matmul",
    )(x, gain.reshape(1, d), w, wg_hi, wg_lo, post)


def _gates_body(pre_ref, aux_ref, col_ref, row_ref, *, lanes):
    pre = pre_ref[...] + aux_ref[0:1, :]
    scale = aux_ref[1:2, :]
    lane = lax.broadcasted_iota(I32, pre.shape, 1)
    sp = _softplus(pre)
    vals = jnp.zeros_like(pre)
    cum_mask = jnp.zeros(pre.shape, jnp.bool_)
    for lo, hi, kind, cum in lanes:
        m = (lane >= lo) & (lane < hi)
        if kind == "id":
            v = pre
        elif kind == "logsig":
            v = pre - sp
        elif kind == "softplus":
            v = scale * sp
        else:
            v = _sigmoid(pre)
        vals = jnp.where(m, v, vals)
        if cum:
            cum_mask = cum_mask | m
    pos = lax.broadcasted_iota(I32, pre.shape, 0) & (CHUNK - 1)
    c = vals
    k = 1
    while k < CHUNK:
        c = c + jnp.where(pos >= k, pltpu.roll(c, k, 0), 0.0)
        k *= 2
    out = jnp.where(cum_mask, c, vals)
    col_ref[...] = out
    row_ref[...] = out.T


def _gates(pre, aux, lanes, tm=2048):
    t = pre.shape[0]
    tm = min(tm, t)
    return pl.pallas_call(
        functools.partial(_gates_body, lanes=lanes),
        grid=(t // tm,),
        in_specs=[
            pl.BlockSpec((tm, LANES), lambda i: (i, 0)),
            pl.BlockSpec((8, LANES), lambda i: (0, 0)),
        ],
        out_specs=[
            pl.BlockSpec((tm, LANES), lambda i: (i, 0)),
            pl.BlockSpec((LANES, tm), lambda i: (0, i)),
        ],
        out_shape=[jax.ShapeDtypeStruct((t, LANES), F32), jax.ShapeDtypeStruct((LANES, t), F32)],
        compiler_params=_cparams("parallel"),
        name="gates",
    )(pre, aux)


def _mlstm_body(q_ref, k_ref, v_ref, og_ref, gcol_ref, grow_ref, mn_ref, o_ref, c_scr, n_scr, m_scr, *, ts):
    @pl.when(pl.program_id(1) == 0)
    def _():
        c_scr[...] = jnp.zeros_like(c_scr)
        n_scr[...] = jnp.zeros_like(n_scr)
        m_scr[...] = jnp.zeros_like(m_scr)

    scale = MLSTM_DQK ** -0.5
    nh, nc = MLSTM_HEADS, ts // CHUNK
    nb = nh * nc
    q = jnp.stack([q_ref[:, h * MLSTM_DQK:(h + 1) * MLSTM_DQK] for h in range(nh)], axis=0).reshape(nb, CHUNK, MLSTM_DQK)
    k = jnp.stack([k_ref[:, h * MLSTM_DQK:(h + 1) * MLSTM_DQK] for h in range(nh)], axis=0).reshape(nb, CHUNK, MLSTM_DQK)
    v = jnp.stack([v_ref[:, h * MLSTM_DV:(h + 1) * MLSTM_DV] for h in range(nh)], axis=0).reshape(nb, CHUNK, MLSTM_DV)
    gcol = gcol_ref[...]
    grow = grow_ref[...]
    li_col = jnp.stack([gcol[:, h:h + 1] for h in range(nh)], axis=0).reshape(nb, CHUNK, 1)
    b_col = jnp.stack([gcol[:, nh + h:nh + h + 1] for h in range(nh)], axis=0).reshape(nb, CHUNK, 1)
    li_row = jnp.stack([grow[0:nh, c * CHUNK:(c + 1) * CHUNK] for c in range(nc)], axis=1).reshape(nb, 1, CHUNK)
    b_row = jnp.stack([grow[nh:2 * nh, c * CHUNK:(c + 1) * CHUNK] for c in range(nc)], axis=1).reshape(nb, 1, CHUNK)

    b_end = b_col[:, CHUNK - 1:CHUNK, :]
    g_end = b_end - b_col + li_col
    m_loc = jnp.max(g_end, axis=1, keepdims=True)
    kw = k.astype(F32) * jnp.exp(g_end - m_loc)
    upd_c = _bdot_tn(kw.astype(BF16), v).reshape(nh, nc, MLSTM_DQK, MLSTM_DV)
    upd_n = jnp.sum(kw, axis=1, keepdims=True).reshape(nh, nc, 1, MLSTM_DQK)
    log_d = jnp.where(_tri(CHUNK)[None], b_col - b_row + li_row, -jnp.inf)
    max_d = jnp.max(log_d, axis=2, keepdims=True)
    s_raw = _bdot_nt(q, k)

    b_end4 = b_end.reshape(nh, nc, 1, 1)
    m_loc4 = m_loc.reshape(nh, nc, 1, 1)
    m_prev = m_scr[...][:, 0:1, 0:1]
    m_prevs, s_olds, s_upds = [], [], []
    for c in range(nc):
        m_prevs.append(m_prev)
        m_new = jnp.maximum(b_end4[:, c] + m_prev, m_loc4[:, c])
        s_olds.append(jnp.exp(b_end4[:, c] + m_prev - m_new))
        s_upds.append(jnp.exp(m_loc4[:, c] - m_new))
        m_prev = m_new
    m_scr[...] = jnp.broadcast_to(m_prev, m_scr.shape)

    inter_log = b_col + jnp.stack(m_prevs, axis=1).reshape(nb, 1, 1)
    m_t = jnp.maximum(inter_log, max_d)
    w_inter = jnp.exp(inter_log - m_t)
    scores = s_raw * (scale * jnp.exp(log_d - m_t))
    num = _bdot(scores.astype(BF16), v)
    den = jnp.sum(scores, axis=2, keepdims=True)

    q4 = q.reshape(nh, nc, CHUNK, MLSTM_DQK)
    c_state = c_scr[...]
    n_state = n_scr[...][:, 0:1, :]
    qcs, qns = [], []
    for c in range(nc):
        qcs.append(_bdot(q4[:, c], c_state.astype(BF16)))
        qns.append(jnp.sum(q4[:, c].astype(F32) * n_state, axis=2, keepdims=True))
        c_state = s_olds[c] * c_state + s_upds[c] * upd_c[:, c]
        n_state = s_olds[c] * n_state + s_upds[c] * upd_n[:, c]
    c_scr[...] = c_state
    n_scr[...] = jnp.broadcast_to(n_state, n_scr.shape)

    num = num + w_inter * (jnp.stack(qcs, axis=1).reshape(nb, CHUNK, MLSTM_DV) * scale)
    den = den + w_inter * (jnp.stack(qns, axis=1).reshape(nb, CHUNK, 1) * scale)
    hout = num / jnp.maximum(jnp.abs(den), jnp.exp(-m_t))
    hn = _rms_rows(hout).reshape(nh, ts, MLSTM_DV)
    for h in range(nh):
        cols = slice(h * MLSTM_DV, (h + 1) * MLSTM_DV)
        o_ref[:, cols] = (hn[h] * mn_ref[:, cols] * _sigmoid(og_ref[:, cols].astype(F32))).astype(o_ref.dtype)


def _mlstm(p, gcol, grow, m_norm, batch, seq, ts=256):
    t = batch * seq
    ts = min(ts, seq)
    nt = seq // ts
    qk_w = MLSTM_HEADS * MLSTM_DQK
    v_w = MLSTM_HEADS * MLSTM_DV
    row = lambda b, i: b * nt + i
    return pl.pallas_call(
        functools.partial(_mlstm_body, ts=ts),
        grid=(batch, nt),
        in_specs=[
            pl.BlockSpec((ts, qk_w), lambda b, i: (row(b, i), 0)),
            pl.BlockSpec((ts, qk_w), lambda b, i: (row(b, i), 1)),
            pl.BlockSpec((ts, v_w), lambda b, i: (row(b, i), 1)),
            pl.BlockSpec((ts, v_w), lambda b, i: (row(b, i), 2)),
            pl.BlockSpec((ts, LANES), lambda b, i: (row(b, i), 0)),
            pl.BlockSpec((8, ts), lambda b, i: (0, row(b, i))),
            pl.BlockSpec((1, v_w), lambda b, i: (0, 0)),
        ],
        out_specs=pl.BlockSpec((ts, v_w), lambda b, i: (row(b, i), 0)),
        out_shape=jax.ShapeDtypeStruct((t, v_w), BF16),
        scratch_shapes=[
            pltpu.VMEM((MLSTM_HEADS, MLSTM_DQK, MLSTM_DV), F32),
            pltpu.VMEM((MLSTM_HEADS, 8, MLSTM_DQK), F32),
            pltpu.VMEM((MLSTM_HEADS, 8, LANES), F32),
        ],
        compiler_params=_cparams("parallel", "arbitrary"),
        name="mlstm",
    )(p, p, p, p, gcol, grow, m_norm.reshape(1, v_w))


def _causal_conv(x_ref, tail_ref, w_ref, cols, ts, bias=None):
    xp = jnp.concatenate([tail_ref[:, cols], x_ref[:, cols].astype(F32)], axis=0)
    y = w_ref[CONV_WIDTH - 1:CONV_WIDTH, cols] * xp[8:, :]
    for j in range(CONV_WIDTH - 1):
        y = y + w_ref[j:j + 1, cols] * pltpu.roll(xp, CONV_WIDTH - 1 - j, 0)[8:, :]
    if bias is not None:
        y = y + bias[:, cols]
    return y


def _gdn_body(x_ref, z_ref, gcol_ref, grow_ref, cw_ref, gn_ref, o_ref, tail_scr, s_scr, *, ts):
    @pl.when(pl.program_id(1) == 0)
    def _():
        tail_scr[...] = jnp.zeros_like(tail_scr)
        s_scr[...] = jnp.zeros_like(s_scr)

    nh, nc = GDN_HEADS, ts // CHUNK
    nb = nh * nc
    qk_w = nh * GDN_DK

    def conv_heads(base):
        return jnp.stack([_silu(_causal_conv(x_ref, tail_scr, cw_ref, slice(base + h * GDN_DK, base + (h + 1) * GDN_DK), ts))
                          for h in range(nh)], axis=0)

    q = conv_heads(0)
    k = conv_heads(qk_w)
    v = conv_heads(2 * qk_w).reshape(nb, CHUNK, GDN_DV)
    q = (q * lax.rsqrt(jnp.sum(q * q, axis=-1, keepdims=True) + EPS) * (GDN_DK ** -0.5)).reshape(nb, CHUNK, GDN_DK)
    k = (k * lax.rsqrt(jnp.sum(k * k, axis=-1, keepdims=True) + EPS)).reshape(nb, CHUNK, GDN_DK)

    gcol = gcol_ref[...]
    grow = grow_ref[...]
    g_col = jnp.stack([gcol[:, 16 + h:17 + h] for h in range(nh)], axis=0).reshape(nb, CHUNK, 1)
    beta = jnp.stack([gcol[:, 24 + h:25 + h] for h in range(nh)], axis=0).reshape(nb, CHUNK, 1)
    g_row = jnp.stack([grow[:, c * CHUNK:(c + 1) * CHUNK] for c in range(nc)], axis=1).reshape(nb, 1, CHUNK)
    g_end = g_col[:, CHUNK - 1:CHUNK, :]

    tri = _tri(CHUNK)[None]
    tri_strict = _tri(CHUNK, strict=True)[None]
    eye = (lax.broadcasted_iota(I32, (CHUNK, CHUNK), 0) == lax.broadcasted_iota(I32, (CHUNK, CHUNK), 1)).astype(F32)[None]
    kb = k.astype(BF16)
    decay = jnp.exp(jnp.where(tri, g_col - g_row, -jnp.inf))
    qk_kk = _bdot_nt(jnp.concatenate([q, k], axis=1).astype(BF16), kb)
    a_mat = jnp.where(tri_strict, beta * qk_kk[:, CHUNK:] * decay, 0.0)
    p = -a_mat
    y = eye + p
    pb = p.astype(BF16)
    p = _bdot(pb, pb)
    for _ in range(4):
        py = _bdot(p.astype(BF16), jnp.concatenate([y, p], axis=2).astype(BF16))
        y = y + py[:, :, :CHUNK]
        p = py[:, :, CHUNK:]
    y = y + _bdot(p.astype(BF16), y.astype(BF16))
    eg = jnp.exp(g_col)
    rhs = jnp.concatenate([beta * v, (beta * eg) * k], axis=2).astype(BF16)
    sol = _bdot(y.astype(BF16), rhs)
    u = sol[:, :, :GDN_DV].reshape(nh, nc, CHUNK, GDN_DV)
    wq = jnp.concatenate([sol[:, :, GDN_DV:], q * eg], axis=1).astype(BF16).reshape(nh, nc, 2 * CHUNK, GDN_DK)
    qk = (qk_kk[:, :CHUNK] * decay).astype(BF16).reshape(nh, nc, CHUNK, CHUNK)
    k_dec = (k * jnp.exp(g_end - g_col)).astype(BF16).reshape(nh, nc, CHUNK, GDN_DK)
    g_tot = jnp.exp(g_end).reshape(nh, nc, 1, 1)

    s = s_scr[...]
    outs = []
    for c in range(nc):
        ws = _bdot(wq[:, c], s.astype(BF16))
        vb = (u[:, c] - ws[:, :CHUNK]).astype(BF16)
        outs.append(ws[:, CHUNK:] + _bdot(qk[:, c], vb))
        s = g_tot[:, c] * s + _bdot_tn(k_dec[:, c], vb)
    s_scr[...] = s

    o = jnp.stack(outs, axis=1).reshape(nh, ts, GDN_DV)
    on = _rms_rows(o) * gn_ref[...]
    for h in range(nh):
        cols = slice(h * GDN_DV, (h + 1) * GDN_DV)
        o_ref[:, cols] = (on[h] * _silu(z_ref[:, cols].astype(F32))).astype(o_ref.dtype)
    tail_scr[...] = x_ref[ts - 8:ts, :].astype(F32)


def _gdn(p, gcol, grow, conv_w, g_norm, batch, seq, ts=256):
    t = batch * seq
    ts = min(ts, seq)
    nt = seq // ts
    qkv_w = 3 * GDN_HEADS * GDN_DK
    v_w = GDN_HEADS * GDN_DV
    row = lambda b, i: b * nt + i
    return pl.pallas_call(
        functools.partial(_gdn_body, ts=ts),
        grid=(batch, nt),
        in_specs=[
            pl.BlockSpec((ts, qkv_w), lambda b, i: (row(b, i), 1)),
            pl.BlockSpec((ts, v_w), lambda b, i: (row(b, i), 6)),
            pl.BlockSpec((ts, LANES), lambda b, i: (row(b, i), 0)),
            pl.BlockSpec((8, ts), lambda b, i: (2, row(b, i))),
            pl.BlockSpec((CONV_WIDTH, qkv_w), lambda b, i: (0, 0)),
            pl.BlockSpec((1, GDN_DV), lambda b, i: (0, 0)),
        ],
        out_specs=pl.BlockSpec((ts, v_w), lambda b, i: (row(b, i), 0)),
        out_shape=jax.ShapeDtypeStruct((t, v_w), BF16),
        scratch_shapes=[
            pltpu.VMEM((8, qkv_w), F32),
            pltpu.VMEM((GDN_HEADS, GDN_DK, GDN_DV), F32),
        ],
        compiler_params=_cparams("parallel", "arbitrary"),
        name="gdn",
    )(p, p, gcol, grow, conv_w, g_norm.reshape(1, GDN_DV))


def _out_proj_body(x_ref, a_ref, b_ref, wa_ref, wb_ref, o_ref):
    o_ref[...] = x_ref[...] + _dot(a_ref[...], wa_ref[...]) + _dot(b_ref[...], wb_ref[...])


def _out_proj(x, a, b, wa, wb, tm=512):
    t, d = x.shape
    tm = min(tm, t)
    ka, kb = a.shape[1], b.shape[1]
    return pl.pallas_call(
        _out_proj_body,
        grid=(t // tm,),
        in_specs=[
            pl.BlockSpec((tm, d), lambda i: (i, 0)),
            pl.BlockSpec((tm, ka), lambda i: (i, 0)),
            pl.BlockSpec((tm, kb), lambda i: (i, 0)),
            pl.BlockSpec((ka, d), lambda i: (0, 0)),
            pl.BlockSpec((kb, d), lambda i: (0, 0)),
        ],
        out_specs=pl.BlockSpec((tm, d), lambda i: (i, 0)),
        out_shape=jax.ShapeDtypeStruct((t, d), F32),
        compiler_params=_cparams("parallel"),
        name="out_proj",
    )(x, a, b, wa, wb)


def _lane_table(entries):
    row = jnp.zeros((LANES,), F32)
    for start, vals in entries:
        row = lax.dynamic_update_slice(row, vals.astype(F32).reshape(-1), (start,))
    return row.reshape(1, LANES)


def _gate_weight(w_in, entries):
    w = jnp.zeros((w_in.shape[0], LANES), F32)
    for start, lo, hi in entries:
        w = lax.dynamic_update_slice(w, w_in[:, lo:hi].astype(F32), (0, start))
    return w


EVEN_GATE_LANES = ((0, 4, "id", False), (4, 8, "logsig", True), (16, 24, "softplus", True), (24, 32, "sigmoid", False))


def _even_mix(xf, batch, seq, gain, w_in, i_bias, f_bias, m_norm, conv_w, a_log, dt_bias, g_norm, w_out):
    w_big = jnp.concatenate([w_in[:, 0:3072], w_in[:, 3080:7176]], axis=1).astype(BF16)
    w_gate = _gate_weight(w_in, ((0, 3072, 3076), (4, 3076, 3080), (16, 7176, 7184), (24, 7184, 7192)))
    aux = jnp.concatenate([
        _lane_table(((0, i_bias), (4, f_bias), (16, dt_bias))),
        _lane_table(((16, -jnp.exp(a_log)),)),
        jnp.zeros((6, LANES), F32)], axis=0)
    p, pre = _norm_matmul(xf, gain, w_big, w_gate, tn=w_big.shape[1] // 4)
    gcol, grow = _gates(pre, aux, EVEN_GATE_LANES)
    hm = _mlstm(p, gcol, grow, m_norm, batch, seq)
    o = _gdn(p, gcol, grow, conv_w, g_norm, batch, seq)
    v_w = MLSTM_HEADS * MLSTM_DV
    return _out_proj(xf, hm, o, w_out[:v_w].astype(BF16), w_out[v_w:].astype(BF16))


ATT_TQ = 256


def _attn_body(q_ref, k0_ref, k1_ref, k2_ref, v0_ref, v1_ref, v2_ref, bias_ref, o_ref):
    i = pl.program_id(1)
    tq = q_ref.shape[0]
    lo_half = lax.broadcasted_iota(I32, (1, LANES), 1) < ATT_DH
    lo_rows = lax.broadcasted_iota(I32, (LANES, 1), 0) < ATT_DH
    k_refs = (k0_ref, k1_ref, k2_ref)
    v_refs = (v0_ref, v1_ref, v2_ref)
    for hp in range(ATT_HEADS // 2):
        cols = slice(hp * LANES, (hp + 1) * LANES)
        q = q_ref[:, cols]
        ks = [r[:, cols] for r in k_refs]
        vs = [r[:, cols] for r in v_refs]
        outs = []
        for hh in range(2):
            own = lo_half if hh == 0 else jnp.logical_not(lo_half)
            qm = jnp.where(own, q, jnp.zeros_like(q))
            s = [_dot_nt(ks[j], qm) + bias_ref[2 * hp + hh, j * tq:(j + 1) * tq, :] for j in range(3)]
            s[0] = jnp.where(i >= 2, s[0], -jnp.inf)
            s[1] = jnp.where(i >= 1, s[1], -jnp.inf)
            m = functools.reduce(jnp.maximum, [jnp.max(sj, axis=0, keepdims=True) for sj in s])
            o = functools.reduce(jnp.add, [_dot_tn(jnp.where(own, vs[j], jnp.ones_like(vs[j])),
                                                   jnp.exp((s[j] - m).astype(BF16))) for j in range(3)])
            outs.append(o / pltpu.roll(o, ATT_DH, 0))
        o_ref[:, cols] = jnp.where(lo_rows, outs[0], outs[1]).T.astype(o_ref.dtype)


def _attn(p, bias, batch, seq, col0):
    t = batch * seq
    tq = ATT_TQ
    nt = seq // tq
    w = ATT_HEADS * ATT_DH
    row = lambda b, i, back: b * nt + jnp.maximum(i - back, 0)
    kv_spec = lambda off, back: pl.BlockSpec((tq, w), lambda b, i: (row(b, i, back), col0 + off))
    return pl.pallas_call(
        _attn_body,
        grid=(batch, nt),
        in_specs=[
            pl.BlockSpec((tq, w), lambda b, i: (row(b, i, 0), col0)),
            kv_spec(1, 2), kv_spec(1, 1), kv_spec(1, 0),
            kv_spec(2, 2), kv_spec(2, 1), kv_spec(2, 0),
            pl.BlockSpec((ATT_HEADS, 3 * tq, tq), lambda b, i: (0, 0, 0)),
        ],
        out_specs=pl.BlockSpec((tq, w), lambda b, i: (row(b, i, 0), 0)),
        out_shape=jax.ShapeDtypeStruct((t, w), BF16),
        compiler_params=_cparams("parallel", "arbitrary"),
        name="band_attn",
    )(p, p, p, p, p, p, p, bias)


def _attn_bias_table(rel_bias):
    pad = ATT_LEFT_CHUNKS * CHUNK
    band = pad + CHUNK
    rel = jnp.arange(CHUNK)[:, None] + pad - jnp.arange(band)[None, :]
    rel_idx = jnp.clip(rel, -(CHUNK - 1), REL_MAX_PAST) + (CHUNK - 1)
    onehot = (rel_idx[None, :, :] == jnp.arange(rel_bias.shape[1])[:, None, None]).astype(F32)
    bias64 = jnp.einsum("hr,rlj->hlj", rel_bias.astype(F32), onehot, precision=HIGHEST)
    nq = ATT_TQ // CHUNK
    rows = [jnp.pad(bias64, ((0, 0), (0, 0), (a * CHUNK, (nq - 1 - a) * CHUNK)), constant_values=-jnp.inf)
            for a in range(nq)]
    return jnp.swapaxes(jnp.concatenate(rows, axis=1), 1, 2)


def _ssd_body(x_ref, z_ref, gcol_ref, grow_ref, cw_ref, cb_ref, dsk_ref, sn_ref, o_ref,
              tail_scr, xc_scr, y_scr, s_scr, *, ts):
    @pl.when(pl.program_id(1) == 0)
    def _():
        tail_scr[...] = jnp.zeros_like(tail_scr)
        s_scr[...] = jnp.zeros_like(s_scr)

    conv_w = x_ref.shape[1]
    for j in range(conv_w // LANES):
        cols = slice(j * LANES, (j + 1) * LANES)
        xc_scr[:, cols] = _silu(_causal_conv(x_ref, tail_scr, cw_ref, cols, ts, bias=cb_ref))

    nc, ng, npair = ts // CHUNK, SSD_GROUPS, SSD_HEADS // 2
    ppg = npair // ng
    nb = npair * nc
    gw = SSD_HPG * SSD_DH
    lo_half = lax.broadcasted_iota(I32, (1, LANES), 1) < SSD_DH
    gcol = gcol_ref[...]
    grow = grow_ref[...]

    def pair_cols(base):
        return jnp.stack([jnp.where(lo_half, gcol[:, base + p:base + p + 1], gcol[:, base + npair + p:base + npair + p + 1])
                          for p in range(npair)], axis=0).reshape(nb, CHUNK, LANES)

    dt = pair_cols(0)
    a_col = pair_cols(SSD_HEADS)
    a_row = jnp.stack([jnp.concatenate([grow[0:npair, c * CHUNK:(c + 1) * CHUNK], grow[npair:, c * CHUNK:(c + 1) * CHUNK]], axis=1)
                       for c in range(nc)], axis=1).reshape(nb, 1, LANES)
    a_end = a_col[:, CHUNK - 1:CHUNK, :]
    xs = jnp.stack([xc_scr[:, p * LANES:(p + 1) * LANES] for p in range(npair)], axis=0).reshape(nb, CHUNK, LANES)
    dsk = jnp.stack([dsk_ref[:, p * LANES:(p + 1) * LANES] for p in range(npair)], axis=0)
    dsk = jnp.broadcast_to(dsk[:, None], (npair, nc, 1, LANES)).reshape(nb, 1, LANES)
    bm = jnp.stack([xc_scr[:, SSD_INNER + g * SSD_STATE:SSD_INNER + (g + 1) * SSD_STATE] for g in range(ng)], axis=0)
    cm = jnp.stack([xc_scr[:, SSD_INNER + SSD_BC_W + g * SSD_STATE:SSD_INNER + SSD_BC_W + (g + 1) * SSD_STATE]
                    for g in range(ng)], axis=0)
    bm = bm.astype(BF16).reshape(ng * nc, CHUNK, SSD_STATE)
    cm = cm.astype(BF16).reshape(ng * nc, CHUNK, SSD_STATE)

    tri2 = (lax.broadcasted_iota(I32, (CHUNK, LANES), 0) >= (lax.broadcasted_iota(I32, (CHUNK, LANES), 1) & (SSD_DH - 1)))[None]
    decay = jnp.exp(jnp.where(tri2, a_col - a_row, -jnp.inf))
    cb = _bdot_nt(cm, bm)
    cb2 = jnp.concatenate([cb, cb], axis=2).reshape(ng, 1, nc, CHUNK, LANES)
    cb2 = jnp.broadcast_to(cb2, (ng, ppg, nc, CHUNK, LANES)).reshape(nb, CHUNK, LANES)
    xd = xs * dt
    rhs = jnp.concatenate([jnp.where(lo_half, xd, 0.0), jnp.where(lo_half, 0.0, xd)], axis=1).astype(BF16)
    y_in = (_bdot((cb2 * decay).astype(BF16), rhs) + dsk * xs).reshape(ng, ppg, nc, CHUNK, LANES)
    ea = jnp.exp(a_col).reshape(ng, ppg, nc, CHUNK, LANES)
    dxd = (xd * jnp.exp(a_end - a_col)).astype(BF16).reshape(ng, ppg, nc, CHUNK, LANES)
    cdec = jnp.exp(a_end).reshape(ng, ppg, nc, 1, LANES)
    bm = bm.reshape(ng, nc, CHUNK, SSD_STATE)
    cm = cm.reshape(ng, nc, CHUNK, SSD_STATE)

    s = s_scr[...]
    for c in range(nc):
        rows = slice(c * CHUNK, (c + 1) * CHUNK)
        y_off = _bdot(cm[:, c], s.astype(BF16))
        for j in range(ppg):
            yj = y_in[:, j, c] + y_off[:, :, j * LANES:(j + 1) * LANES] * ea[:, j, c]
            for g in range(ng):
                y_scr[rows, (g * ppg + j) * LANES:(g * ppg + j + 1) * LANES] = yj[g]
        dx = jnp.concatenate([dxd[:, j, c] for j in range(ppg)], axis=2)
        cd = jnp.concatenate([cdec[:, j, c] for j in range(ppg)], axis=2)
        s = s * cd + _bdot_tn(bm[:, c], dx)
    s_scr[...] = s

    for g in range(SSD_GROUPS):
        cols = slice(g * gw, (g + 1) * gw)
        yg = y_scr[:, cols] * _silu(z_ref[:, cols].astype(F32))
        o_ref[:, cols] = (_rms_rows(yg) * sn_ref[:, cols]).astype(o_ref.dtype)
    tail_scr[...] = x_ref[ts - 8:ts, :].astype(F32)


def _ssd(p, gcol, grow, conv_w, conv_b, d_skip, s_norm, batch, seq, ts=256):
    t = batch * seq
    ts = min(ts, seq)
    nt = seq // ts
    cw = SSD_INNER + 2 * SSD_BC_W
    row = lambda b, i: b * nt + i
    return pl.pallas_call(
        functools.partial(_ssd_body, ts=ts),
        grid=(batch, nt),
        in_specs=[
            pl.BlockSpec((ts, cw), lambda b, i: (row(b, i), 0)),
            pl.BlockSpec((ts, SSD_INNER), lambda b, i: (row(b, i), 3)),
            pl.BlockSpec((ts, LANES), lambda b, i: (row(b, i), 0)),
            pl.BlockSpec((SSD_HEADS, ts), lambda b, i: (1, row(b, i))),
            pl.BlockSpec((CONV_WIDTH, cw), lambda b, i: (0, 0)),
            pl.BlockSpec((1, cw), lambda b, i: (0, 0)),
            pl.BlockSpec((1, SSD_INNER), lambda b, i: (0, 0)),
            pl.BlockSpec((1, SSD_INNER), lambda b, i: (0, 0)),
        ],
        out_specs=pl.BlockSpec((ts, SSD_INNER), lambda b, i: (row(b, i), 0)),
        out_shape=jax.ShapeDtypeStruct((t, SSD_INNER), BF16),
        scratch_shapes=[
            pltpu.VMEM((8, cw), F32),
            pltpu.VMEM((ts, cw), F32),
            pltpu.VMEM((ts, SSD_INNER), F32),
            pltpu.VMEM((SSD_GROUPS, SSD_STATE, SSD_HPG * SSD_DH), F32),
        ],
        compiler_params=_cparams("parallel", "arbitrary"),
        name="ssd",
    )(p, p, gcol, grow, conv_w, conv_b.reshape(1, cw), jnp.repeat(d_skip, SSD_DH).reshape(1, SSD_INNER),
      s_norm.reshape(1, SSD_INNER))


ODD_GATE_LANES = ((0, 32, "softplus", False), (32, 64, "softplus", True))


def _odd_mix(xf, batch, seq, gain, w_in, q_gain, k_gain, rel_bias, conv_w, conv_b, a_log, dt_bias, d_skip, s_norm, w_out):
    w_big = jnp.concatenate([w_in[:, 5120:8192], w_in[:, 0:3072], w_in[:, 3072:5120]], axis=1).astype(BF16)
    even_odd = lambda a: jnp.concatenate([a[..., 0::2], a[..., 1::2]], axis=-1)
    w_dt = even_odd(w_in[:, 8192:8224])
    w_gate = _gate_weight(jnp.concatenate([w_dt, w_dt], axis=1), ((0, 0, 2 * SSD_HEADS),))
    aux = jnp.concatenate([
        _lane_table(((0, even_odd(dt_bias)), (SSD_HEADS, even_odd(dt_bias)))),
        _lane_table(((0, jnp.ones((SSD_HEADS,), F32)), (SSD_HEADS, -jnp.exp(even_odd(a_log))))),
        jnp.zeros((6, LANES), F32)], axis=0)
    a_w = ATT_HEADS * ATT_DH
    post = jnp.ones((1, w_big.shape[1]), F32)
    post = lax.dynamic_update_slice(post, jnp.tile(q_gain.astype(F32) * ATT_DH ** -0.5, ATT_HEADS).reshape(1, a_w), (0, 3072))
    post = lax.dynamic_update_slice(post, jnp.tile(k_gain.astype(F32), ATT_HEADS).reshape(1, a_w), (0, 3072 + a_w))
    p, pre = _norm_matmul(xf, gain, w_big, w_gate, post=post, norm_tiles=(3, 4))
    gcol, grow = _gates(pre, aux, ODD_GATE_LANES)
    attn = _attn(p, _attn_bias_table(rel_bias), batch, seq, col0=3072 // a_w)
    y = _ssd(p, gcol, grow, conv_w, conv_b, d_skip, s_norm, batch, seq)
    return _out_proj(xf, attn, y, w_out[:a_w].astype(BF16), w_out[a_w:].astype(BF16))


ROW_W = D_MODEL + LANES
MOE_TM = 512
DMA_ROWS = 512


def _router_body(x_ref, g_ref, wh_ref, wl_ref, b_ref, rows_ref, info_ref, cnt_ref, carry_scr):
    @pl.when(pl.program_id(0) == 0)
    def _():
        carry_scr[...] = jnp.zeros_like(carry_scr)

    h = _rms_rows(x_ref[...]) * g_ref[...]
    h_hi, h_lo = _split_bf16(h)
    logits = _dot(h_hi, wh_ref[...]) + _dot(h_lo, wh_ref[...]) + _dot(h_hi, wl_ref[...]) + b_ref[...]
    tm = logits.shape[0]
    n_log = MOE_GROUPS + MOE_GROUPS * MOE_EPG + 4
    lt = logits.T[0:n_log, :]
    row = lax.broadcasted_iota(I32, lt.shape, 0).astype(F32)
    ninf = -jnp.inf

    def first_max(v):
        m = jnp.max(v, axis=0, keepdims=True)
        return m, jnp.min(jnp.where(v == m, row, float(LANES)), axis=0, keepdims=True)

    glog = jnp.where(row < MOE_GROUPS, lt, ninf)
    gmax, gidx = first_max(glog)
    grp_w = 1.0 / jnp.sum(jnp.exp(glog - gmax), axis=0, keepdims=True)
    e_lo = MOE_GROUPS + gidx * MOE_EPG
    elog = jnp.where((row >= e_lo) & (row < e_lo + MOE_EPG), lt, ninf)
    v1, i1 = first_max(elog)
    v2, i2 = first_max(jnp.where(row == i1, ninf, elog))
    e21 = jnp.exp(v2 - v1)
    w1 = 1.0 / (1.0 + e21)
    comb = grp_w * (jnp.where(row == i1, w1, 0.0) + jnp.where(row == i2, e21 * w1, 0.0))
    local = functools.reduce(jnp.add, [comb[MOE_GROUPS + g * MOE_EPG:MOE_GROUPS + (g + 1) * MOE_EPG, :]
                                       for g in range(MOE_GROUPS)])
    local = jnp.concatenate([local, jnp.zeros((LANES - MOE_EPG, tm), F32)], axis=0).T
    rows_ref[...] = jnp.concatenate([h, local], axis=1)

    row8 = lax.broadcasted_iota(I32, (8, tm), 0).astype(F32)
    onehot = jnp.where(row8 == gidx, 1.0, 0.0)
    earlier =jnp.where(lax.broadcasted_iota(I32, (tm, tm), 0) < lax.broadcasted_iota(I32, (tm, tm), 1), 1.0, 0.0).astype(BF16)
    carry = carry_scr[:, 0:1]
    before = _dot(onehot.astype(BF16), earlier) + carry
    rank = jnp.sum(onehot * before, axis=0, keepdims=True)
    carry = carry + jnp.sum(onehot, axis=1, keepdims=True)
    carry_scr[...] = jnp.broadcast_to(carry, carry_scr.shape)
    cnt_ref[...] = jnp.broadcast_to(carry, cnt_ref.shape)
    info_ref[...] = jnp.where(row8 == 0, gidx, jnp.where(row8 == 1, rank, 0.0))


def _router(x, gain, w, b, tm=256):
    t, d = x.shape
    tm = min(tm, t)
    w_hi, w_lo = _split_bf16(w)
    return pl.pallas_call(
        _router_body,
        grid=(t // tm,),
        in_specs=[
            pl.BlockSpec((tm, d), lambda i: (i, 0)),
            pl.BlockSpec((1, d), lambda i: (0, 0)),
            pl.BlockSpec((d, LANES), lambda i: (0, 0)),
            pl.BlockSpec((d, LANES), lambda i: (0, 0)),
            pl.BlockSpec((1, LANES), lambda i: (0, 0)),
        ],
        out_specs=[
            pl.BlockSpec((tm, ROW_W), lambda i: (i, 0)),
            pl.BlockSpec((8, tm), lambda i: (0, i)),
            pl.BlockSpec((8, LANES), lambda i: (0, 0)),
        ],
        out_shape=[
            jax.ShapeDtypeStruct((t, ROW_W), F32),
            jax.ShapeDtypeStruct((8, t), F32),
            jax.ShapeDtypeStruct((8, LANES), F32),
        ],
        scratch_shapes=[pltpu.VMEM((8, LANES), F32)],
        compiler_params=_cparams("arbitrary"),
        name="moe_router",
    )(x, gain.reshape(1, d), w_hi, w_lo, b)


def _dispatch_body(pos_ref, pad_ref, rows_ref, out_ref, zero_scr, sem, zero_sem):
    i = pl.program_id(0)
    base = i * DMA_ROWS

    @pl.when(i == 0)
    def _():
        zero_scr[...] = jnp.zeros_like(zero_scr)
        n_tiles = out_ref.shape[0] // MOE_TM
        fills = []
        for g in range(MOE_GROUPS):
            first, length = pad_ref[g], pad_ref[MOE_GROUPS + g]
            end = first + length
            for r in range(7):
                fills.append((r < jnp.bitwise_and(length, 7), first + r, 1))
            piece = MOE_TM // 2
            while piece >= 8:
                fills.append((jnp.bitwise_and(length, piece) != 0,
                              pl.multiple_of(end - jnp.bitwise_and(length, -piece), 8), piece))
                piece //= 2
        for j in range(MOE_GROUPS):
            tile = pad_ref[2 * MOE_GROUPS] + j
            fills.append((tile < n_tiles, pl.multiple_of(tile * MOE_TM, 8), MOE_TM))

        def fill_copy(row, n):
            return pltpu.make_async_copy(zero_scr.at[pl.ds(0, n)], out_ref.at[pl.ds(row, n)], zero_sem)

        for cond, row, n in fills:
            pl.when(cond)(lambda row=row, n=n: fill_copy(row, n).start())
        for cond, row, n in fills:
            pl.when(cond)(lambda row=row, n=n: fill_copy(row, n).wait())

    def row_copy(r):
        return pltpu.make_async_copy(rows_ref.at[pl.ds(r, 1)], out_ref.at[pl.ds(pos_ref[base + r], 1)], sem)

    def start(g, carry):
        for k in range(8):
            row_copy(g * 8 + k).start(priority=k % 2)
        return carry

    lax.fori_loop(0, DMA_ROWS // 8, start, 0)
    pltpu.make_async_copy(rows_ref, out_ref.at[pl.ds(0, DMA_ROWS)], sem).wait()


def _dispatch(pos, pad, rows, n_sorted):
    t, w = rows.shape
    return pl.pallas_call(
        _dispatch_body,
        grid_spec=pltpu.PrefetchScalarGridSpec(
            num_scalar_prefetch=2,
            grid=(t // DMA_ROWS,),
            in_specs=[pl.BlockSpec((DMA_ROWS, w), lambda i, pos, pad: (i, 0))],
            out_specs=pl.BlockSpec(memory_space=pl.ANY),
            scratch_shapes=[pltpu.VMEM((MOE_TM, w), rows.dtype), pltpu.SemaphoreType.DMA(()), pltpu.SemaphoreType.DMA(())],
        ),
        out_shape=jax.ShapeDtypeStruct((n_sorted, w), rows.dtype),
        compiler_params=_cparams("arbitrary"),
        name="moe_dispatch",
    )(pos, pad, rows)


def _experts_body(tile_group_ref, n_used_ref, xs_ref, wg_ref, wu_ref, wd_ref, y_ref, act_scr):
    del tile_group_ref
    i = pl.program_id(0)

    @pl.when(i < n_used_ref[0])
    def _():
        x = xs_ref[:, :D_MODEL].astype(BF16)
        comb = xs_ref[:, D_MODEL:]
        for e in range(MOE_EPG):
            a = _dot(x, wg_ref[e])
            u = _dot(x, wu_ref[e])
            act_scr[:, e * D_EXPERT:(e + 1) * D_EXPERT] = (_silu(a) * u * comb[:, e:e + 1]).astype(BF16)
        y_ref[...] = _dot(act_scr[...], wd_ref[...].reshape(MOE_EPG * D_EXPERT, D_MODEL))

    @pl.when(i >= n_used_ref[0])
    def _():
        y_ref[...] = jnp.zeros_like(y_ref)


def _experts(tile_group, n_used, xs, w_gate, w_up, w_down):
    n_sorted = xs.shape[0]
    tm = MOE_TM
    w_in_spec = pl.BlockSpec((None, MOE_EPG, D_MODEL, D_EXPERT), lambda i, tg, nu: (tg[i], 0, 0, 0))
    return pl.pallas_call(
        _experts_body,
        grid_spec=pltpu.PrefetchScalarGridSpec(
            num_scalar_prefetch=2,
            grid=(n_sorted // tm,),
            in_specs=[
                pl.BlockSpec((tm, ROW_W), lambda i, tg, nu: (i, 0)),
                w_in_spec,
                w_in_spec,
                pl.BlockSpec((None, MOE_EPG, D_EXPERT, D_MODEL), lambda i, tg, nu: (tg[i], 0, 0, 0)),
            ],
            out_specs=pl.BlockSpec((tm, D_MODEL), lambda i, tg, nu: (i, 0)),
            scratch_shapes=[pltpu.VMEM((tm, MOE_EPG * D_EXPERT), BF16)],
        ),
        out_shape=jax.ShapeDtypeStruct((n_sorted, D_MODEL), F32),
        compiler_params=_cparams("arbitrary"),
        name="moe_experts",
    )(tile_group, n_used, xs, w_gate, w_up, w_down)


def _combine_body(pos_ref, x_ref, y_ref, o_ref, buf, sem):
    base = pl.program_id(0) * DMA_ROWS

    def row_copy(r):
        return pltpu.make_async_copy(y_ref.at[pl.ds(pos_ref[base + r], 1)], buf.at[pl.ds(r, 1)], sem)

    def start(g, carry):
        for k in range(8):
            row_copy(g * 8 + k).start(priority=k % 2)
        return carry

    lax.fori_loop(0, DMA_ROWS // 8, start, 0)
    pltpu.make_async_copy(y_ref.at[pl.ds(0, DMA_ROWS)], buf, sem).wait()
    o_ref[...] = x_ref[...] + buf[...]


def _combine(pos, x, y):
    t, d = x.shape
    return pl.pallas_call(
        _combine_body,
        grid_spec=pltpu.PrefetchScalarGridSpec(
            num_scalar_prefetch=1,
            grid=(t // DMA_ROWS,),
            in_specs=[
                pl.BlockSpec((DMA_ROWS, d), lambda i, pos: (i, 0)),
                pl.BlockSpec(memory_space=pl.ANY),
            ],
            out_specs=pl.BlockSpec((DMA_ROWS, d), lambda i, pos: (i, 0)),
            scratch_shapes=[pltpu.VMEM((DMA_ROWS, d), F32), pltpu.SemaphoreType.DMA(())],
        ),
        out_shape=jax.ShapeDtypeStruct((t, d), F32),
        compiler_params=_cparams("arbitrary"),
        name="moe_combine",
    )(pos, x, y)


def _moe(xf, gain, wg, bg, we, be, w_gate, w_up, w_down):
    t = xf.shape[0]
    w_r = jnp.zeros((D_MODEL, LANES), F32)
    w_r = lax.dynamic_update_slice(w_r, wg.astype(F32), (0, 0))
    w_r = lax.dynamic_update_slice(w_r, we.astype(F32), (0, MOE_GROUPS))
    b_r = _lane_table(((0, bg), (MOE_GROUPS, be)))
    rows, info, cnt = _router(xf, gain, w_r, b_r)

    gidx = info[0].astype(I32)
    rank = info[1].astype(I32)
    counts = cnt[:MOE_GROUPS, 0].astype(I32)
    padded = ((counts + MOE_TM - 1) // MOE_TM) * MOE_TM
    ends = jnp.cumsum(padded)
    starts = ends - padded
    pos = rank + sum(jnp.where(gidx == g, starts[g], 0) for g in range(MOE_GROUPS))
    n_sorted = t + MOE_GROUPS * MOE_TM
    n_tiles = n_sorted // MOE_TM
    tile_start = jnp.arange(n_tiles, dtype=I32) * MOE_TM
    tile_group = jnp.minimum(jnp.sum(tile_start[:, None] >= ends[None, :], axis=1), MOE_GROUPS - 1).astype(I32)
    n_used = (ends[-1:] // MOE_TM).astype(I32)

    pad = jnp.concatenate([starts + counts, padded - counts, n_used]).astype(I32)
    xs = _dispatch(pos, pad, rows, n_sorted)
    ys = _experts(tile_group, n_used, xs, w_gate.astype(BF16), w_up.astype(BF16), w_down.astype(BF16))
    return _combine(pos, xf, ys)


def kernel(x, norm_mix, norm_ffn, even_w_in, mlstm_i_bias, mlstm_f_bias, mlstm_norm, gdn_conv, gdn_A_log, gdn_dt_bias,
           gdn_norm, even_w_out, odd_w_in, att_q_norm, att_k_norm, att_rel_bias, ssd_conv, ssd_conv_bias, ssd_A_log,
           ssd_dt_bias, ssd_D, ssd_norm, odd_w_out, router_group_w, router_group_b, router_expert_w, router_expert_b,
           moe_w_gate, moe_w_up, moe_w_down):
    batch, seq, d = x.shape
    xf = x.reshape(batch * seq, d)
    depth = norm_mix.shape[0]
    for layer in range(depth):
        i = layer // 2
        if layer % 2 == 0:
            xf = _even_mix(xf, batch, seq, norm_mix[layer], even_w_in[i], mlstm_i_bias[i], mlstm_f_bias[i], mlstm_norm[i],
                           gdn_conv[i], gdn_A_log[i], gdn_dt_bias[i], gdn_norm[i], even_w_out[i])
        else:
            xf = _odd_mix(xf, batch, seq, norm_mix[layer], odd_w_in[i], att_q_norm[i], att_k_norm[i], att_rel_bias[i],
                          ssd_conv[i], ssd_conv_bias[i], ssd_A_log[i], ssd_dt_bias[i], ssd_D[i], ssd_norm[i], odd_w_out[i])
        xf = _moe(xf, norm_ffn[layer], router_group_w[layer], router_group_b[layer], router_expert_w[layer],
                  router_expert_b[layer], moe_w_gate[layer], moe_w_up[layer], moe_w_down[layer])
    return xf.reshape(batch, seq, d)
```

```python
import functools

import jax
import jax.numpy as jnp
from jax import lax
from jax.experimental import pallas as pl
from jax.experimental.pallas import tpu as pltpu

F32 = jnp.float32
BF16 = jnp.bfloat16
I32 = jnp.int32
HIGHEST = lax.Precision.HIGHEST

D_MODEL = 1024
CHUNK = 64
CONV_WIDTH = 4
EPS = 1e-6
LANES = 128

MLSTM_HEADS, MLSTM_DQK, MLSTM_DV = 4, 128, 256
GDN_HEADS, GDN_DK, GDN_DV = 8, 128, 128
ATT_HEADS, ATT_DH, ATT_LEFT_CHUNKS, REL_MAX_PAST = 16, 64, 8, 256
SSD_HEADS, SSD_DH, SSD_STATE, SSD_GROUPS = 32, 64, 128, 4
SSD_HPG = SSD_HEADS // SSD_GROUPS
SSD_INNER = SSD_HEADS * SSD_DH
SSD_BC_W = SSD_GROUPS * SSD_STATE
MOE_GROUPS, MOE_EPG, D_EXPERT = 4, 8, 256

VMEM_LIMIT = 48 * 1024 * 1024


def _cparams(*sem):
    return pltpu.CompilerParams(dimension_semantics=sem, vmem_limit_bytes=VMEM_LIMIT)


def _softplus(x):
    return jnp.maximum(x, 0.0) + jnp.log1p(jnp.exp(-jnp.abs(x)))


def _sigmoid(x):
    return 1.0 / (1.0 + jnp.exp(-x))


def _silu(x):
    return x * _sigmoid(x)


def _rms_rows(x):
    return x * lax.rsqrt(jnp.mean(x * x, axis=-1, keepdims=True) + EPS)


def _dot(a, b):
    return jnp.dot(a, b, preferred_element_type=F32)


def _dot_nt(a, b):
    return lax.dot_general(a, b, (((1,), (1,)), ((), ())), preferred_element_type=F32)


def _dot_tn(a, b):
    return lax.dot_general(a, b, (((0,), (0,)), ((), ())), preferred_element_type=F32)


def _bdot(a, b):
    return lax.dot_general(a, b, (((2,), (1,)), ((0,), (0,))), preferred_element_type=F32)


def _bdot_nt(a, b):
    return lax.dot_general(a, b, (((2,), (2,)), ((0,), (0,))), preferred_element_type=F32)


def _bdot_tn(a, b):
    return lax.dot_general(a, b, (((1,), (1,)), ((0,), (0,))), preferred_element_type=F32)


def _tri(n, strict=False):
    r = lax.broadcasted_iota(I32, (n, n), 0)
    c = lax.broadcasted_iota(I32, (n, n), 1)
    return (r > c) if strict else (r >= c)


def _split_bf16(x):
    hi = x.astype(BF16)
    return hi, (x - hi.astype(F32)).astype(BF16)


def _pair_head_norm(x):
    lo = lax.broadcasted_iota(I32, x.shape, 1) < (LANES // 2)
    sq = x * x
    ms_lo = jnp.sum(jnp.where(lo, sq, 0.0), axis=-1, keepdims=True)
    ms_hi = jnp.sum(jnp.where(lo, 0.0, sq), axis=-1, keepdims=True)
    return x * lax.rsqrt(jnp.where(lo, ms_lo, ms_hi) * (2.0 / LANES) + EPS)


def _norm_matmul_body(x_ref, g_ref, w_ref, wg_ref, post_ref, o_ref, gate_ref, h_scr, *, norm_tiles):
    j = pl.program_id(1)

    @pl.when(j == 0)
    def _():
        hi, lo = _split_bf16(_rms_rows(x_ref[...]) * g_ref[...])
        h_scr[...] = hi
        parts = _dot(hi, wg_ref[...]) + _dot(lo, wg_ref[...])
        gate_ref[...] = parts[:, :LANES] + parts[:, LANES:]

    acc = _dot(h_scr[...], w_ref[...])
    if not norm_tiles:
        o_ref[...] = acc.astype(o_ref.dtype)
        return
    is_norm = functools.reduce(jnp.logical_or, [j == tile for tile in norm_tiles])

    @pl.when(is_norm)
    def _():
        half = LANES // 2
        same_head = (lax.broadcasted_iota(I32, (LANES, LANES), 0) >= half) == (lax.broadcasted_iota(I32, (LANES, LANES), 1) >= half)
        avg = jnp.where(same_head, 1.0 / half, 0.0).astype(BF16)
        for s in range(acc.shape[1] // LANES):
            cols = slice(s * LANES, (s + 1) * LANES)
            blk = acc[:, cols]
            ms = _dot((blk * blk).astype(BF16), avg)
            o_ref[:, cols] = (blk * lax.rsqrt(ms + EPS) * post_ref[:, cols]).astype(o_ref.dtype)

    @pl.when(jnp.logical_not(is_norm))
    def _():
        o_ref[...] = acc.astype(o_ref.dtype)


def _norm_matmul(x, gain, w, w_gate, post=None, norm_tiles=(), tm=1024, tn=1024):
    t, d = x.shape
    n = w.shape[1]
    tm = min(tm, t)
    wg_parts = jnp.concatenate(_split_bf16(w_gate), axis=1)
    if post is None:
        post = jnp.ones((1, n), F32)
    return pl.pallas_call(
        functools.partial(_norm_matmul_body, norm_tiles=tuple(norm_tiles)),
        grid=(t // tm, n // tn),
        in_specs=[
            pl.BlockSpec((tm, d), lambda i, j: (i, 0)),
            pl.BlockSpec((1, d), lambda i, j: (0, 0)),
            pl.BlockSpec((d, tn), lambda i, j: (0, j)),
            pl.BlockSpec((d, 2 * LANES), lambda i, j: (0, 0)),
            pl.BlockSpec((1, tn), lambda i, j: (0, j)),
        ],
        out_specs=[
            pl.BlockSpec((tm, tn), lambda i, j: (i, j)),
            pl.BlockSpec((tm, LANES), lambda i, j: (i, 0)),
        ],
        out_shape=[jax.ShapeDtypeStruct((t, n), BF16), jax.ShapeDtypeStruct((t, LANES), F32)],
        scratch_shapes=[pltpu.VMEM((tm, d), BF16)],
        compiler_params=_cparams("parallel", "arbitrary"),
        name="norm_matmul",
    )(x, gain.reshape(1, d), w, wg_parts, post)


def _gates_body(pre_ref, aux_ref, col_ref, row_ref, *, lanes):
    pre = pre_ref[...] + aux_ref[0:1, :]
    scale = aux_ref[1:2, :]
    lane = lax.broadcasted_iota(I32, pre.shape, 1)
    sp = _softplus(pre)
    vals = jnp.zeros_like(pre)
    cum_mask = jnp.zeros(pre.shape, jnp.bool_)
    for lo, hi, kind, cum in lanes:
        m = (lane >= lo) & (lane < hi)
        if kind == "id":
            v = pre
        elif kind == "logsig":
            v = pre - sp
        elif kind == "softplus":
            v = scale * sp
        else:
            v = _sigmoid(pre)
        vals = jnp.where(m, v, vals)
        if cum:
            cum_mask = cum_mask | m
    pos = lax.broadcasted_iota(I32, pre.shape, 0) & (CHUNK - 1)
    c = vals
    k = 1
    while k < CHUNK:
        c = c + jnp.where(pos >= k, pltpu.roll(c, k, 0), 0.0)
        k *= 2
    out = jnp.where(cum_mask, c, vals)
    col_ref[...] = out
    row_ref[...] = out.T


def _gates(pre, aux, lanes, tm=2048):
    t = pre.shape[0]
    tm = min(tm, t)
    return pl.pallas_call(
        functools.partial(_gates_body, lanes=lanes),
        grid=(t // tm,),
        in_specs=[
            pl.BlockSpec((tm, LANES), lambda i: (i, 0)),
            pl.BlockSpec((8, LANES), lambda i: (0, 0)),
        ],
        out_specs=[
            pl.BlockSpec((tm, LANES), lambda i: (i, 0)),
            pl.BlockSpec((LANES, tm), lambda i: (0, i)),
        ],
        out_shape=[jax.ShapeDtypeStruct((t, LANES), F32), jax.ShapeDtypeStruct((LANES, t), F32)],
        compiler_params=_cparams("parallel"),
        name="gates",
    )(pre, aux)


def _mlstm_body(q_ref, k_ref, v_ref, og_ref, gcol_ref, grow_ref, mn_ref, o_ref, c_scr, n_scr, m_scr, *, ts):
    @pl.when(pl.program_id(1) == 0)
    def _():
        c_scr[...] = jnp.zeros_like(c_scr)
        n_scr[...] = jnp.zeros_like(n_scr)
        m_scr[...] = jnp.zeros_like(m_scr)

    scale = MLSTM_DQK ** -0.5
    nh, nc = MLSTM_HEADS, ts // CHUNK
    nb = nh * nc
    q = jnp.stack([q_ref[:, h * MLSTM_DQK:(h + 1) * MLSTM_DQK] for h in range(nh)], axis=0).reshape(nb, CHUNK, MLSTM_DQK)
    k = jnp.stack([k_ref[:, h * MLSTM_DQK:(h + 1) * MLSTM_DQK] for h in range(nh)], axis=0).reshape(nb, CHUNK, MLSTM_DQK)
    v = jnp.stack([v_ref[:, h * MLSTM_DV:(h + 1) * MLSTM_DV] for h in range(nh)], axis=0).reshape(nb, CHUNK, MLSTM_DV)
    gcol = gcol_ref[...]
    grow = grow_ref[...]
    li_col = jnp.stack([gcol[:, h:h + 1] for h in range(nh)], axis=0).reshape(nb, CHUNK, 1)
    b_col = jnp.stack([gcol[:, nh + h:nh + h + 1] for h in range(nh)], axis=0).reshape(nb, CHUNK, 1)
    li_row = jnp.stack([grow[0:nh, c * CHUNK:(c + 1) * CHUNK] for c in range(nc)], axis=1).reshape(nb, 1, CHUNK)
    b_row = jnp.stack([grow[nh:2 * nh, c * CHUNK:(c + 1) * CHUNK] for c in range(nc)], axis=1).reshape(nb, 1, CHUNK)

    b_end = b_col[:, CHUNK - 1:CHUNK, :]
    g_end = b_end - b_col + li_col
    m_loc = jnp.max(g_end, axis=1, keepdims=True)
    kw = k.astype(F32) * jnp.exp(g_end - m_loc)
    upd_c = _bdot_tn(kw.astype(BF16), v).reshape(nh, nc, MLSTM_DQK, MLSTM_DV)
    upd_n = jnp.sum(kw, axis=1, keepdims=True).reshape(nh, nc, 1, MLSTM_DQK)
    log_d = jnp.where(_tri(CHUNK)[None], b_col - b_row + li_row, -jnp.inf)
    max_d = jnp.max(log_d, axis=2, keepdims=True)
    s_raw = _bdot_nt(q, k)

    b_end4 = b_end.reshape(nh, nc, 1, 1)
    m_loc4 = m_loc.reshape(nh, nc, 1, 1)
    m_prev = m_scr[...][:, 0:1, 0:1]
    m_prevs, s_olds, s_upds = [], [], []
    for c in range(nc):
        m_prevs.append(m_prev)
        m_new = jnp.maximum(b_end4[:, c] + m_prev, m_loc4[:, c])
        s_olds.append(jnp.exp(b_end4[:, c] + m_prev - m_new))
        s_upds.append(jnp.exp(m_loc4[:, c] - m_new))
        m_prev = m_new
    m_scr[...] = jnp.broadcast_to(m_prev, m_scr.shape)

    inter_log = b_col + jnp.stack(m_prevs, axis=1).reshape(nb, 1, 1)
    m_t = jnp.maximum(inter_log, max_d)
    w_inter = jnp.exp(inter_log - m_t)
    scores = s_raw * (scale * jnp.exp(log_d - m_t))
    num = _bdot(scores.astype(BF16), v)
    den = jnp.sum(scores, axis=2, keepdims=True)

    q4 = q.reshape(nh, nc, CHUNK, MLSTM_DQK)
    c_state = c_scr[...]
    n_state = n_scr[...][:, 0:1, :]
    qcs, qns = [], []
    for c in range(nc):
        qcs.append(_bdot(q4[:, c], c_state.astype(BF16)))
        qns.append(jnp.sum(q4[:, c].astype(F32) * n_state, axis=2, keepdims=True))
        c_state = s_olds[c] * c_state + s_upds[c] * upd_c[:, c]
        n_state = s_olds[c] * n_state + s_upds[c] * upd_n[:, c]
    c_scr[...] = c_state
    n_scr[...] = jnp.broadcast_to(n_state, n_scr.shape)

    num = num + w_inter * (jnp.stack(qcs, axis=1).reshape(nb, CHUNK, MLSTM_DV) * scale)
    den = den + w_inter * (jnp.stack(qns, axis=1).reshape(nb, CHUNK, 1) * scale)
    hout = num / jnp.maximum(jnp.abs(den), jnp.exp(-m_t))
    hn = _rms_rows(hout).reshape(nh, ts, MLSTM_DV)
    for h in range(nh):
        cols = slice(h * MLSTM_DV, (h + 1) * MLSTM_DV)
        o_ref[:, cols] = (hn[h] * mn_ref[:, cols] * _sigmoid(og_ref[:, cols].astype(F32))).astype(o_ref.dtype)


def _mlstm(p, gcol, grow, m_norm, batch, seq, ts=512):
    t = batch * seq
    ts = min(ts, seq)
    nt = seq // ts
    qk_w = MLSTM_HEADS * MLSTM_DQK
    v_w = MLSTM_HEADS * MLSTM_DV
    row = lambda b, i: b * nt + i
    return pl.pallas_call(
        functools.partial(_mlstm_body, ts=ts),
        grid=(batch, nt),
        in_specs=[
            pl.BlockSpec((ts, qk_w), lambda b, i: (row(b, i), 0)),
            pl.BlockSpec((ts, qk_w), lambda b, i: (row(b, i), 1)),
            pl.BlockSpec((ts, v_w), lambda b, i: (row(b, i), 1)),
            pl.BlockSpec((ts, v_w), lambda b, i: (row(b, i), 2)),
            pl.BlockSpec((ts, LANES), lambda b, i: (row(b, i), 0)),
            pl.BlockSpec((8, ts), lambda b, i: (0, row(b, i))),
            pl.BlockSpec((1, v_w), lambda b, i: (0, 0)),
        ],
        out_specs=pl.BlockSpec((ts, v_w), lambda b, i: (row(b, i), 0)),
        out_shape=jax.ShapeDtypeStruct((t, v_w), BF16),
        scratch_shapes=[
            pltpu.VMEM((MLSTM_HEADS, MLSTM_DQK, MLSTM_DV), F32),
            pltpu.VMEM((MLSTM_HEADS, 8, MLSTM_DQK), F32),
            pltpu.VMEM((MLSTM_HEADS, 8, LANES), F32),
        ],
        compiler_params=_cparams("parallel", "arbitrary"),
        name="mlstm",
    )(p, p, p, p, gcol, grow, m_norm.reshape(1, v_w))


def _causal_conv(x_ref, tail_ref, w_ref, cols, ts, bias=None):
    xp = jnp.concatenate([tail_ref[:, cols], x_ref[:, cols].astype(F32)], axis=0)
    y = w_ref[CONV_WIDTH - 1:CONV_WIDTH, cols] * xp[8:, :]
    for j in range(CONV_WIDTH - 1):
        y = y + w_ref[j:j + 1, cols] * pltpu.roll(xp, CONV_WIDTH - 1 - j, 0)[8:, :]
    if bias is not None:
        y = y + bias[:, cols]
    return y


def _gdn_body(x_ref, z_ref, gcol_ref, grow_ref, cw_ref, gn_ref, o_ref, tail_scr, s_scr, *, ts):
    @pl.when(pl.program_id(1) == 0)
    def _():
        tail_scr[...] = jnp.zeros_like(tail_scr)
        s_scr[...] = jnp.zeros_like(s_scr)

    nh, nc = GDN_HEADS, ts // CHUNK
    nb = nh * nc
    qk_w = nh * GDN_DK

    def conv_heads(base):
        return jnp.stack([_silu(_causal_conv(x_ref, tail_scr, cw_ref, slice(base + h * GDN_DK, base + (h + 1) * GDN_DK), ts))
                          for h in range(nh)], axis=0)

    q = conv_heads(0)
    k = conv_heads(qk_w)
    v = conv_heads(2 * qk_w).reshape(nb, CHUNK, GDN_DV)
    q = (q * lax.rsqrt(jnp.sum(q * q, axis=-1, keepdims=True) + EPS) * (GDN_DK ** -0.5)).reshape(nb, CHUNK, GDN_DK)
    k = (k * lax.rsqrt(jnp.sum(k * k, axis=-1, keepdims=True) + EPS)).reshape(nb, CHUNK, GDN_DK)

    gcol = gcol_ref[...]
    grow = grow_ref[...]
    g_col = jnp.stack([gcol[:, 16 + h:17 + h] for h in range(nh)], axis=0).reshape(nb, CHUNK, 1)
    beta = jnp.stack([gcol[:, 24 + h:25 + h] for h in range(nh)], axis=0).reshape(nb, CHUNK, 1)
    g_row = jnp.stack([grow[:, c * CHUNK:(c + 1) * CHUNK] for c in range(nc)], axis=1).reshape(nb, 1, CHUNK)
    g_end = g_col[:, CHUNK - 1:CHUNK, :]

    tri = _tri(CHUNK)[None]
    tri_strict = _tri(CHUNK, strict=True)[None]
    eye = (lax.broadcasted_iota(I32, (CHUNK, CHUNK), 0) == lax.broadcasted_iota(I32, (CHUNK, CHUNK), 1)).astype(F32)[None]
    kb = k.astype(BF16)
    decay = jnp.exp(jnp.where(tri, g_col - g_row, -jnp.inf))
    qk_kk = _bdot_nt(jnp.concatenate([q, k], axis=1).astype(BF16), kb)
    a_mat = jnp.where(tri_strict, beta * qk_kk[:, CHUNK:] * decay, 0.0)
    p = -a_mat
    y = eye + p
    pb = p.astype(BF16)
    p = _bdot(pb, pb)
    for _ in range(4):
        py = _bdot(p.astype(BF16), jnp.concatenate([y, p], axis=2).astype(BF16))
        y = y + py[:, :, :CHUNK]
        p = py[:, :, CHUNK:]
    y = y + _bdot(p.astype(BF16), y.astype(BF16))
    eg = jnp.exp(g_col)
    rhs = jnp.concatenate([beta * v, (beta * eg) * k], axis=2).astype(BF16)
    sol = _bdot(y.astype(BF16), rhs)
    u = sol[:, :, :GDN_DV].reshape(nh, nc, CHUNK, GDN_DV)
    wq = jnp.concatenate([sol[:, :, GDN_DV:], q * eg], axis=1).astype(BF16).reshape(nh, nc, 2 * CHUNK, GDN_DK)
    qk = (qk_kk[:, :CHUNK] * decay).astype(BF16).reshape(nh, nc, CHUNK, CHUNK)
    k_dec = (k * jnp.exp(g_end - g_col)).astype(BF16).reshape(nh, nc, CHUNK, GDN_DK)
    g_tot = jnp.exp(g_end).reshape(nh, nc, 1, 1)

    s = s_scr[...]
    outs = []
    for c in range(nc):
        ws = _bdot(wq[:, c], s.astype(BF16))
        vb = (u[:, c] - ws[:, :CHUNK]).astype(BF16)
        outs.append(ws[:, CHUNK:] + _bdot(qk[:, c], vb))
        s = g_tot[:, c] * s + _bdot_tn(k_dec[:, c], vb)
    s_scr[...] = s

    o = jnp.stack(outs, axis=1).reshape(nh, ts, GDN_DV)
    on = _rms_rows(o) * gn_ref[...]
    for h in range(nh):
        cols = slice(h * GDN_DV, (h + 1) * GDN_DV)
        o_ref[:, cols] = (on[h] * _silu(z_ref[:, cols].astype(F32))).astype(o_ref.dtype)
    tail_scr[...] = x_ref[ts - 8:ts, :].astype(F32)


def _gdn(p, gcol, grow, conv_w, g_norm, batch, seq, ts=256):
    t = batch * seq
    ts = min(ts, seq)
    nt = seq // ts
    qkv_w = 3 * GDN_HEADS * GDN_DK
    v_w = GDN_HEADS * GDN_DV
    row = lambda b, i: b * nt + i
    return pl.pallas_call(
        functools.partial(_gdn_body, ts=ts),
        grid=(batch, nt),
        in_specs=[
            pl.BlockSpec((ts, qkv_w), lambda b, i: (row(b, i), 1)),
            pl.BlockSpec((ts, v_w), lambda b, i: (row(b, i), 6)),
            pl.BlockSpec((ts, LANES), lambda b, i: (row(b, i), 0)),
            pl.BlockSpec((8, ts), lambda b, i: (2, row(b, i))),
            pl.BlockSpec((CONV_WIDTH, qkv_w), lambda b, i: (0, 0)),
            pl.BlockSpec((1, GDN_DV), lambda b, i: (0, 0)),
        ],
        out_specs=pl.BlockSpec((ts, v_w), lambda b, i: (row(b, i), 0)),
        out_shape=jax.ShapeDtypeStruct((t, v_w), BF16),
        scratch_shapes=[
            pltpu.VMEM((8, qkv_w), F32),
            pltpu.VMEM((GDN_HEADS, GDN_DK, GDN_DV), F32),
        ],
        compiler_params=_cparams("parallel", "arbitrary"),
        name="gdn",
    )(p, p, gcol, grow, conv_w, g_norm.reshape(1, GDN_DV))


def _out_proj_body(x_ref, a_ref, b_ref, wa_ref, wb_ref, o_ref):
    o_ref[...] = x_ref[...] + _dot(a_ref[...], wa_ref[...]) + _dot(b_ref[...], wb_ref[...])


def _out_proj(x, a, b, wa, wb, tm=512):
    t, d = x.shape
    tm = min(tm, t)
    ka, kb = a.shape[1], b.shape[1]
    return pl.pallas_call(
        _out_proj_body,
        grid=(t // tm,),
        in_specs=[
            pl.BlockSpec((tm, d), lambda i: (i, 0)),
            pl.BlockSpec((tm, ka), lambda i: (i, 0)),
            pl.BlockSpec((tm, kb), lambda i: (i, 0)),
            pl.BlockSpec((ka, d), lambda i: (0, 0)),
            pl.BlockSpec((kb, d), lambda i: (0, 0)),
        ],
        out_specs=pl.BlockSpec((tm, d), lambda i: (i, 0)),
        out_shape=jax.ShapeDtypeStruct((t, d), F32),
        compiler_params=_cparams("parallel"),
        name="out_proj",
    )(x, a, b, wa, wb)


def _lane_table(entries):
    row = jnp.zeros((LANES,), F32)
    for start, vals in entries:
        row = lax.dynamic_update_slice(row, vals.astype(F32).reshape(-1), (start,))
    return row.reshape(1, LANES)


def _gate_weight(w_in, entries):
    w = jnp.zeros((w_in.shape[0], LANES), F32)
    for start, lo, hi in entries:
        w = lax.dynamic_update_slice(w, w_in[:, lo:hi].astype(F32), (0, start))
    return w


EVEN_GATE_LANES = ((0, 4, "id", False), (4, 8, "logsig", True), (16, 24, "softplus", True), (24, 32, "sigmoid", False))


def _even_mix(xf, batch, seq, gain, w_in, i_bias, f_bias, m_norm, conv_w, a_log, dt_bias, g_norm, w_out):
    w_big = jnp.concatenate([w_in[:, 0:3072], w_in[:, 3080:7176]], axis=1).astype(BF16)
    w_gate = _gate_weight(w_in, ((0, 3072, 3076), (4, 3076, 3080), (16, 7176, 7184), (24, 7184, 7192)))
    aux = jnp.concatenate([
        _lane_table(((0, i_bias), (4, f_bias), (16, dt_bias))),
        _lane_table(((16, -jnp.exp(a_log)),)),
        jnp.zeros((6, LANES), F32)], axis=0)
    p, pre = _norm_matmul(xf, gain, w_big, w_gate, tn=w_big.shape[1] // 4)
    gcol, grow = _gates(pre, aux, EVEN_GATE_LANES)
    hm = _mlstm(p, gcol, grow, m_norm, batch, seq)
    o = _gdn(p, gcol, grow, conv_w, g_norm, batch, seq)
    v_w = MLSTM_HEADS * MLSTM_DV
    return _out_proj(xf, hm, o, w_out[:v_w].astype(BF16), w_out[v_w:].astype(BF16))


ATT_TQ = 256


def _attn_body(q_ref, k0_ref, k1_ref, k2_ref, v0_ref, v1_ref, v2_ref, bias_ref, o_ref):
    i = pl.program_id(1)
    tq = q_ref.shape[0]
    lo_half = lax.broadcasted_iota(I32, (1, LANES), 1) < ATT_DH
    lo_rows = lax.broadcasted_iota(I32, (LANES, 1), 0) < ATT_DH
    k_refs = (k0_ref, k1_ref, k2_ref)
    v_refs = (v0_ref, v1_ref, v2_ref)
    for hp in range(ATT_HEADS // 2):
        cols = slice(hp * LANES, (hp + 1) * LANES)
        q = q_ref[:, cols]
        ks = [r[:, cols] for r in k_refs]
        vs = [r[:, cols] for r in v_refs]
        outs = []
        for hh in range(2):
            own = lo_half if hh == 0 else jnp.logical_not(lo_half)
            qm = jnp.where(own, q, jnp.zeros_like(q))
            s = [_dot_nt(ks[j], qm) + bias_ref[2 * hp + hh, j * tq:(j + 1) * tq, :] for j in range(3)]
            s[0] = jnp.where(i >= 2, s[0], -jnp.inf)
            s[1] = jnp.where(i >= 1, s[1], -jnp.inf)
            m = functools.reduce(jnp.maximum, [jnp.max(sj, axis=0, keepdims=True) for sj in s])
            o = functools.reduce(jnp.add, [_dot_tn(jnp.where(own, vs[j], jnp.ones_like(vs[j])),
                                                   jnp.exp((s[j] - m).astype(BF16))) for j in range(3)])
            outs.append(o / pltpu.roll(o, ATT_DH, 0))
        o_ref[:, cols] = jnp.where(lo_rows, outs[0], outs[1]).T.astype(o_ref.dtype)


def _attn(p, bias, batch, seq, col0):
    t = batch * seq
    tq = ATT_TQ
    nt = seq // tq
    w = ATT_HEADS * ATT_DH
    row = lambda b, i, back: b * nt + jnp.maximum(i - back, 0)
    kv_spec = lambda off, back: pl.BlockSpec((tq, w), lambda b, i: (row(b, i, back), col0 + off))
    return pl.pallas_call(
        _attn_body,
        grid=(batch, nt),
        in_specs=[
            pl.BlockSpec((tq, w), lambda b, i: (row(b, i, 0), col0)),
            kv_spec(1, 2), kv_spec(1, 1), kv_spec(1, 0),
            kv_spec(2, 2), kv_spec(2, 1), kv_spec(2, 0),
            pl.BlockSpec((ATT_HEADS, 3 * tq, tq), lambda b, i: (0, 0, 0)),
        ],
        out_specs=pl.BlockSpec((tq, w), lambda b, i: (row(b, i, 0), 0)),
        out_shape=jax.ShapeDtypeStruct((t, w), BF16),
        compiler_params=_cparams("parallel", "arbitrary"),
        name="band_attn",
    )(p, p, p, p, p, p, p, bias)


def _attn_bias_table(rel_bias):
    pad = ATT_LEFT_CHUNKS * CHUNK
    band = pad + CHUNK
    rel = jnp.arange(CHUNK)[:, None] + pad - jnp.arange(band)[None, :]
    rel_idx = jnp.clip(rel, -(CHUNK - 1), REL_MAX_PAST) + (CHUNK - 1)
    onehot = (rel_idx[None, :, :] == jnp.arange(rel_bias.shape[1])[:, None, None]).astype(F32)
    bias64 = jnp.einsum("hr,rlj->hlj", rel_bias.astype(F32), onehot, precision=HIGHEST)
    nq = ATT_TQ // CHUNK
    rows = [jnp.pad(bias64, ((0, 0), (0, 0), (a * CHUNK, (nq - 1 - a) * CHUNK)), constant_values=-jnp.inf)
            for a in range(nq)]
    return jnp.swapaxes(jnp.concatenate(rows, axis=1), 1, 2)


def _ssd_body(x_ref, z_ref, gcol_ref, grow_ref, cw_ref, cb_ref, dsk_ref, sn_ref, o_ref,
              tail_scr, xc_scr, y_scr, s_scr, *, ts):
    @pl.when(pl.program_id(1) == 0)
    def _():
        tail_scr[...] = jnp.zeros_like(tail_scr)
        s_scr[...] = jnp.zeros_like(s_scr)

    conv_w = x_ref.shape[1]
    for j in range(conv_w // LANES):
        cols = slice(j * LANES, (j + 1) * LANES)
        xc_scr[:, cols] = _silu(_causal_conv(x_ref, tail_scr, cw_ref, cols, ts, bias=cb_ref))

    nc, ng, npair = ts // CHUNK, SSD_GROUPS, SSD_HEADS // 2
    ppg = npair // ng
    nb = npair * nc
    gw = SSD_HPG * SSD_DH
    lo_half = lax.broadcasted_iota(I32, (1, LANES), 1) < SSD_DH
    gcol = gcol_ref[...]
    grow = grow_ref[...]

    def pair_cols(base):
        return jnp.stack([jnp.where(lo_half, gcol[:, base + p:base + p + 1], gcol[:, base + npair + p:base + npair + p + 1])
                          for p in range(npair)], axis=0).reshape(nb, CHUNK, LANES)

    dt = pair_cols(0)
    a_col = pair_cols(SSD_HEADS)
    a_row = jnp.stack([jnp.concatenate([grow[0:npair, c * CHUNK:(c + 1) * CHUNK], grow[npair:, c * CHUNK:(c + 1) * CHUNK]], axis=1)
                       for c in range(nc)], axis=1).reshape(nb, 1, LANES)
    a_end = a_col[:, CHUNK - 1:CHUNK, :]
    xs = jnp.stack([xc_scr[:, p * LANES:(p + 1) * LANES] for p in range(npair)], axis=0).reshape(nb, CHUNK, LANES)
    dsk = jnp.stack([dsk_ref[:, p * LANES:(p + 1) * LANES] for p in range(npair)], axis=0)
    dsk = jnp.broadcast_to(dsk[:, None], (npair, nc, 1, LANES)).reshape(nb, 1, LANES)
    bm = jnp.stack([xc_scr[:, SSD_INNER + g * SSD_STATE:SSD_INNER + (g + 1) * SSD_STATE] for g in range(ng)], axis=0)
    cm = jnp.stack([xc_scr[:, SSD_INNER + SSD_BC_W + g * SSD_STATE:SSD_INNER + SSD_BC_W + (g + 1) * SSD_STATE]
                    for g in range(ng)], axis=0)
    bm = bm.astype(BF16).reshape(ng * nc, CHUNK, SSD_STATE)
    cm = cm.astype(BF16).reshape(ng * nc, CHUNK, SSD_STATE)

    tri2 = (lax.broadcasted_iota(I32, (CHUNK, LANES), 0) >= (lax.broadcasted_iota(I32, (CHUNK, LANES), 1) & (SSD_DH - 1)))[None]
    decay = jnp.exp(jnp.where(tri2, a_col - a_row, -jnp.inf))
    cb = _bdot_nt(cm, bm)
    cb2 = jnp.concatenate([cb, cb], axis=2).reshape(ng, 1, nc, CHUNK, LANES)
    cb2 = jnp.broadcast_to(cb2, (ng, ppg, nc, CHUNK, LANES)).reshape(nb, CHUNK, LANES)
    xd = xs * dt
    rhs = jnp.concatenate([jnp.where(lo_half, xd, 0.0), jnp.where(lo_half, 0.0, xd)], axis=1).astype(BF16)
    y_in = (_bdot((cb2 * decay).astype(BF16), rhs) + dsk * xs).reshape(ng, ppg, nc, CHUNK, LANES)
    ea = jnp.exp(a_col).reshape(ng, ppg, nc, CHUNK, LANES)
    dxd = (xd * jnp.exp(a_end - a_col)).astype(BF16).reshape(ng, ppg, nc, CHUNK, LANES)
    cdec = jnp.exp(a_end).reshape(ng, ppg, nc, 1, LANES)
    bm = bm.reshape(ng, nc, CHUNK, SSD_STATE)
    cm = cm.reshape(ng, nc, CHUNK, SSD_STATE)

    s = s_scr[...]
    for c in range(nc):
        rows = slice(c * CHUNK, (c + 1) * CHUNK)
        y_off = _bdot(cm[:, c], s.astype(BF16))
        for j in range(ppg):
            yj = y_in[:, j, c] + y_off[:, :, j * LANES:(j + 1) * LANES] * ea[:, j, c]
            for g in range(ng):
                y_scr[rows, (g * ppg + j) * LANES:(g * ppg + j + 1) * LANES] = yj[g]
        dx = jnp.concatenate([dxd[:, j, c] for j in range(ppg)], axis=2)
        cd = jnp.concatenate([cdec[:, j, c] for j in range(ppg)], axis=2)
        s = s * cd + _bdot_tn(bm[:, c], dx)
    s_scr[...] = s

    for g in range(SSD_GROUPS):
        cols = slice(g * gw, (g + 1) * gw)
        yg = y_scr[:, cols] * _silu(z_ref[:, cols].astype(F32))
        o_ref[:, cols] = (_rms_rows(yg) * sn_ref[:, cols]).astype(o_ref.dtype)
    tail_scr[...] = x_ref[ts - 8:ts, :].astype(F32)


def _ssd(p, gcol, grow, conv_w, conv_b, d_skip, s_norm, batch, seq, ts=256):
    t = batch * seq
    ts = min(ts, seq)
    nt = seq // ts
    cw = SSD_INNER + 2 * SSD_BC_W
    row = lambda b, i: b * nt + i
    return pl.pallas_call(
        functools.partial(_ssd_body, ts=ts),
        grid=(batch, nt),
        in_specs=[
            pl.BlockSpec((ts, cw), lambda b, i: (row(b, i), 0)),
            pl.BlockSpec((ts, SSD_INNER), lambda b, i: (row(b, i), 3)),
            pl.BlockSpec((ts, LANES), lambda b, i: (row(b, i), 0)),
            pl.BlockSpec((SSD_HEADS, ts), lambda b, i: (1, row(b, i))),
            pl.BlockSpec((CONV_WIDTH, cw), lambda b, i: (0, 0)),
            pl.BlockSpec((1, cw), lambda b, i: (0, 0)),
            pl.BlockSpec((1, SSD_INNER), lambda b, i: (0, 0)),
            pl.BlockSpec((1, SSD_INNER), lambda b, i: (0, 0)),
        ],
        out_specs=pl.BlockSpec((ts, SSD_INNER), lambda b, i: (row(b, i), 0)),
        out_shape=jax.ShapeDtypeStruct((t, SSD_INNER), BF16),
        scratch_shapes=[
            pltpu.VMEM((8, cw), F32),
            pltpu.VMEM((ts, cw), F32),
            pltpu.VMEM((ts, SSD_INNER), F32),
            pltpu.VMEM((SSD_GROUPS, SSD_STATE, SSD_HPG * SSD_DH), F32),
        ],
        compiler_params=_cparams("parallel", "arbitrary"),
        name="ssd",
    )(p, p, gcol, grow, conv_w, conv_b.reshape(1, cw), jnp.repeat(d_skip, SSD_DH).reshape(1, SSD_INNER),
      s_norm.reshape(1, SSD_INNER))


ODD_GATE_LANES = ((0, 32, "softplus", False), (32, 64, "softplus", True))


def _odd_mix(xf, batch, seq, gain, w_in, q_gain, k_gain, rel_bias, conv_w, conv_b, a_log, dt_bias, d_skip, s_norm, w_out):
    w_big = jnp.concatenate([w_in[:, 5120:8192], w_in[:, 0:3072], w_in[:, 3072:5120]], axis=1).astype(BF16)
    even_odd = lambda a: jnp.concatenate([a[..., 0::2], a[..., 1::2]], axis=-1)
    w_dt = even_odd(w_in[:, 8192:8224])
    w_gate = _gate_weight(jnp.concatenate([w_dt, w_dt], axis=1), ((0, 0, 2 * SSD_HEADS),))
    aux = jnp.concatenate([
        _lane_table(((0, even_odd(dt_bias)), (SSD_HEADS, even_odd(dt_bias)))),
        _lane_table(((0, jnp.ones((SSD_HEADS,), F32)), (SSD_HEADS, -jnp.exp(even_odd(a_log))))),
        jnp.zeros((6, LANES), F32)], axis=0)
    a_w = ATT_HEADS * ATT_DH
    post = jnp.ones((1, w_big.shape[1]), F32)
    post = lax.dynamic_update_slice(post, jnp.tile(q_gain.astype(F32) * ATT_DH ** -0.5, ATT_HEADS).reshape(1, a_w), (0, 3072))
    post = lax.dynamic_update_slice(post, jnp.tile(k_gain.astype(F32), ATT_HEADS).reshape(1, a_w), (0, 3072 + a_w))
    p, pre = _norm_matmul(xf, gain, w_big, w_gate, post=post, norm_tiles=(3, 4))
    gcol, grow = _gates(pre, aux, ODD_GATE_LANES)
    attn = _attn(p, _attn_bias_table(rel_bias), batch, seq, col0=3072 // a_w)
    y = _ssd(p, gcol, grow, conv_w, conv_b, d_skip, s_norm, batch, seq)
    return _out_proj(xf, attn, y, w_out[:a_w].astype(BF16), w_out[a_w:].astype(BF16))


ROW_W = D_MODEL + LANES
MOE_TM = 512
DMA_ROWS = 512


def _router_body(x_ref, g_ref, w_ref, b_ref, rows_ref, info_ref, cnt_ref, carry_scr):
    @pl.when(pl.program_id(0) == 0)
    def _():
        carry_scr[...] = jnp.zeros_like(carry_scr)

    h = _rms_rows(x_ref[...]) * g_ref[...]
    h_hi, h_lo = _split_bf16(h)
    parts = _dot(h_hi, w_ref[...]) + _dot(h_lo, w_ref[...])
    logits = parts[:, :LANES] + parts[:, LANES:] + b_ref[...]
    tm = logits.shape[0]
    n_log = MOE_GROUPS + MOE_GROUPS * MOE_EPG + 4
    lt = logits.T[0:n_log, :]
    row = lax.broadcasted_iota(I32, lt.shape, 0).astype(F32)
    ninf = -jnp.inf

    def first_max(v):
        m = jnp.max(v, axis=0, keepdims=True)
        return m, jnp.min(jnp.where(v == m, row, float(LANES)), axis=0, keepdims=True)

    glog = jnp.where(row < MOE_GROUPS, lt, ninf)
    gmax, gidx = first_max(glog)
    grp_w = 1.0 / jnp.sum(jnp.exp(glog - gmax), axis=0, keepdims=True)
    e_lo = MOE_GROUPS + gidx * MOE_EPG
    elog = jnp.where((row >= e_lo) & (row < e_lo + MOE_EPG), lt, ninf)
    v1, i1 = first_max(elog)
    v2, i2 = first_max(jnp.where(row == i1, ninf, elog))
    e21 = jnp.exp(v2 - v1)
    w1 = 1.0 / (1.0 + e21)
    comb = grp_w * (jnp.where(row == i1, w1, 0.0) + jnp.where(row == i2, e21 * w1, 0.0))
    local = functools.reduce(jnp.add, [comb[MOE_GROUPS + g * MOE_EPG:MOE_GROUPS + (g + 1) * MOE_EPG, :]
                                       for g in range(MOE_GROUPS)])
    local = jnp.concatenate([local, jnp.zeros((LANES - MOE_EPG, tm), F32)], axis=0).T
    rows_ref[...] = jnp.concatenate([h, local], axis=1)

    row8 = lax.broadcasted_iota(I32, (8, tm), 0).astype(F32)
    onehot = jnp.where(row8 == gidx, 1.0, 0.0)
    earlier =jnp.where(lax.broadcasted_iota(I32, (tm, tm), 0) < lax.broadcasted_iota(I32, (tm, tm), 1), 1.0, 0.0).astype(BF16)
    carry = carry_scr[:, 0:1]
    before = _dot(onehot.astype(BF16), earlier) + carry
    rank = jnp.sum(onehot * before, axis=0, keepdims=True)
    carry = carry + jnp.sum(onehot, axis=1, keepdims=True)
    carry_scr[...] = jnp.broadcast_to(carry, carry_scr.shape)
    cnt_ref[...] = jnp.broadcast_to(carry, cnt_ref.shape)
    info_ref[...] = jnp.where(row8 == 0, gidx, jnp.where(row8 == 1, rank, 0.0))


def _router(x, gain, w, b, tm=256):
    t, d = x.shape
    tm = min(tm, t)
    w_parts = jnp.concatenate(_split_bf16(w), axis=1)
    return pl.pallas_call(
        _router_body,
        grid=(t // tm,),
        in_specs=[
            pl.BlockSpec((tm, d), lambda i: (i, 0)),
            pl.BlockSpec((1, d), lambda i: (0, 0)),
            pl.BlockSpec((d, 2 * LANES), lambda i: (0, 0)),
            pl.BlockSpec((1, LANES), lambda i: (0, 0)),
        ],
        out_specs=[
            pl.BlockSpec((tm, ROW_W), lambda i: (i, 0)),
            pl.BlockSpec((8, tm), lambda i: (0, i)),
            pl.BlockSpec((8, LANES), lambda i: (0, 0)),
        ],
        out_shape=[
            jax.ShapeDtypeStruct((t, ROW_W), F32),
            jax.ShapeDtypeStruct((8, t), F32),
            jax.ShapeDtypeStruct((8, LANES), F32),
        ],
        scratch_shapes=[pltpu.VMEM((8, LANES), F32)],
        compiler_params=_cparams("arbitrary"),
        name="moe_router",
    )(x, gain.reshape(1, d), w_parts, b)


def _dispatch_body(pos_ref, pad_ref, rows_ref, out_ref, zero_scr, sem, zero_sem):
    i = pl.program_id(0)
    base = i * DMA_ROWS

    @pl.when(i == 0)
    def _():
        zero_scr[...] = jnp.zeros_like(zero_scr)
        n_tiles = out_ref.shape[0] // MOE_TM
        fills = []
        for g in range(MOE_GROUPS):
            first, length = pad_ref[g], pad_ref[MOE_GROUPS + g]
            end = first + length
            for r in range(7):
                fills.append((r < jnp.bitwise_and(length, 7), first + r, 1))
            piece = MOE_TM // 2
            while piece >= 8:
                fills.append((jnp.bitwise_and(length, piece) != 0,
                              pl.multiple_of(end - jnp.bitwise_and(length, -piece), 8), piece))
                piece //= 2
        for j in range(MOE_GROUPS):
            tile = pad_ref[2 * MOE_GROUPS] + j
            fills.append((tile < n_tiles, pl.multiple_of(tile * MOE_TM, 8), MOE_TM))

        def fill_copy(row, n):
            return pltpu.make_async_copy(zero_scr.at[pl.ds(0, n)], out_ref.at[pl.ds(row, n)], zero_sem)

        for cond, row, n in fills:
            pl.when(cond)(lambda row=row, n=n: fill_copy(row, n).start())
        for cond, row, n in fills:
            pl.when(cond)(lambda row=row, n=n: fill_copy(row, n).wait())

    def row_copy(r):
        return pltpu.make_async_copy(rows_ref.at[pl.ds(r, 1)], out_ref.at[pl.ds(pos_ref[base + r], 1)], sem)

    def start(g, carry):
        for k in range(8):
            row_copy(g * 8 + k).start(priority=k % 2)
        return carry

    lax.fori_loop(0, DMA_ROWS // 8, start, 0)
    pltpu.make_async_copy(rows_ref, out_ref.at[pl.ds(0, DMA_ROWS)], sem).wait()


def _dispatch(pos, pad, rows, n_sorted):
    t, w = rows.shape
    return pl.pallas_call(
        _dispatch_body,
        grid_spec=pltpu.PrefetchScalarGridSpec(
            num_scalar_prefetch=2,
            grid=(t // DMA_ROWS,),
            in_specs=[pl.BlockSpec((DMA_ROWS, w), lambda i, pos, pad: (i, 0))],
            out_specs=pl.BlockSpec(memory_space=pl.ANY),
            scratch_shapes=[pltpu.VMEM((MOE_TM, w), rows.dtype), pltpu.SemaphoreType.DMA(()), pltpu.SemaphoreType.DMA(())],
        ),
        out_shape=jax.ShapeDtypeStruct((n_sorted, w), rows.dtype),
        compiler_params=_cparams("arbitrary"),
        name="moe_dispatch",
    )(pos, pad, rows)


def _experts_body(tile_group_ref, n_used_ref, xs_ref, wg_ref, wu_ref, wd_ref, y_ref, act_scr):
    del tile_group_ref
    i = pl.program_id(0)

    @pl.when(i < n_used_ref[0])
    def _():
        x = xs_ref[:, :D_MODEL].astype(BF16)
        comb = xs_ref[:, D_MODEL:]
        for e in range(MOE_EPG):
            a = _dot(x, wg_ref[e])
            u = _dot(x, wu_ref[e])
            act_scr[:, e * D_EXPERT:(e + 1) * D_EXPERT] = (_silu(a) * u * comb[:, e:e + 1]).astype(BF16)
        y_ref[...] = _dot(act_scr[...], wd_ref[...].reshape(MOE_EPG * D_EXPERT, D_MODEL))

    @pl.when(i >= n_used_ref[0])
    def _():
        y_ref[...] = jnp.zeros_like(y_ref)


def _experts(tile_group, n_used, xs, w_gate, w_up, w_down):
    n_sorted = xs.shape[0]
    tm = MOE_TM
    w_in_spec = pl.BlockSpec((None, MOE_EPG, D_MODEL, D_EXPERT), lambda i, tg, nu: (tg[i], 0, 0, 0))
    return pl.pallas_call(
        _experts_body,
        grid_spec=pltpu.PrefetchScalarGridSpec(
            num_scalar_prefetch=2,
            grid=(n_sorted // tm,),
            in_specs=[
                pl.BlockSpec((tm, ROW_W), lambda i, tg, nu: (i, 0)),
                w_in_spec,
                w_in_spec,
                pl.BlockSpec((None, MOE_EPG, D_EXPERT, D_MODEL), lambda i, tg, nu: (tg[i], 0, 0, 0)),
            ],
            out_specs=pl.BlockSpec((tm, D_MODEL), lambda i, tg, nu: (i, 0)),
            scratch_shapes=[pltpu.VMEM((tm, MOE_EPG * D_EXPERT), BF16)],
        ),
        out_shape=jax.ShapeDtypeStruct((n_sorted, D_MODEL), F32),
        compiler_params=_cparams("arbitrary"),
        name="moe_experts",
    )(tile_group, n_used, xs, w_gate, w_up, w_down)


def _combine_body(pos_ref, x_ref, y_ref, o_ref, buf, sem):
    base = pl.program_id(0) * DMA_ROWS

    def row_copy(r):
        return pltpu.make_async_copy(y_ref.at[pl.ds(pos_ref[base + r], 1)], buf.at[pl.ds(r, 1)], sem)

    def start(g, carry):
        for k in range(8):
            row_copy(g * 8 + k).start(priority=k % 2)
        return carry

    lax.fori_loop(0, DMA_ROWS // 8, start, 0)
    pltpu.make_async_copy(y_ref.at[pl.ds(0, DMA_ROWS)], buf, sem).wait()
    o_ref[...] = x_ref[...] + buf[...]


def _combine(pos, x, y):
    t, d = x.shape
    return pl.pallas_call(
        _combine_body,
        grid_spec=pltpu.PrefetchScalarGridSpec(
            num_scalar_prefetch=1,
            grid=(t // DMA_ROWS,),
            in_specs=[
                pl.BlockSpec((DMA_ROWS, d), lambda i, pos: (i, 0)),
                pl.BlockSpec(memory_space=pl.ANY),
            ],
            out_specs=pl.BlockSpec((DMA_ROWS, d), lambda i, pos: (i, 0)),
            scratch_shapes=[pltpu.VMEM((DMA_ROWS, d), F32), pltpu.SemaphoreType.DMA(())],
        ),
        out_shape=jax.ShapeDtypeStruct((t, d), F32),
        compiler_params=_cparams("arbitrary"),
        name="moe_combine",
    )(pos, x, y)


def _moe(xf, gain, wg, bg, we, be, w_gate, w_up, w_down):
    t = xf.shape[0]
    w_r = jnp.zeros((D_MODEL, LANES), F32)
    w_r = lax.dynamic_update_slice(w_r, wg.astype(F32), (0, 0))
    w_r = lax.dynamic_update_slice(w_r, we.astype(F32), (0, MOE_GROUPS))
    b_r = _lane_table(((0, bg), (MOE_GROUPS, be)))
    rows, info, cnt = _router(xf, gain, w_r, b_r)

    gidx = info[0].astype(I32)
    rank = info[1].astype(I32)
    counts = cnt[:MOE_GROUPS, 0].astype(I32)
    padded = ((counts + MOE_TM - 1) // MOE_TM) * MOE_TM
    ends = jnp.cumsum(padded)
    starts = ends - padded
    pos = rank + sum(jnp.where(gidx == g, starts[g], 0) for g in range(MOE_GROUPS))
    n_sorted = t + MOE_GROUPS * MOE_TM
    n_tiles = n_sorted // MOE_TM
    tile_start = jnp.arange(n_tiles, dtype=I32) * MOE_TM
    tile_group = jnp.minimum(jnp.sum(tile_start[:, None] >= ends[None, :], axis=1), MOE_GROUPS - 1).astype(I32)
    n_used = (ends[-1:] // MOE_TM).astype(I32)

    pad = jnp.concatenate([starts + counts, padded - counts, n_used]).astype(I32)
    xs = _dispatch(pos, pad, rows, n_sorted)
    ys = _experts(tile_group, n_used, xs, w_gate.astype(BF16), w_up.astype(BF16), w_down.astype(BF16))
    return _combine(pos, xf, ys)


def kernel(x, norm_mix, norm_ffn, even_w_in, mlstm_i_bias, mlstm_f_bias, mlstm_norm, gdn_conv, gdn_A_log, gdn_dt_bias,
           gdn_norm, even_w_out, odd_w_in, att_q_norm, att_k_norm, att_rel_bias, ssd_conv, ssd_conv_bias, ssd_A_log,
           ssd_dt_bias, ssd_D, ssd_norm, odd_w_out, router_group_w, router_group_b, router_expert_w, router_expert_b,
           moe_w_gate, moe_w_up, moe_w_down):
    batch, seq, d = x.shape
    xf = x.reshape(batch * seq, d)
    depth = norm_mix.shape[0]
    for layer in range(depth):
        i = layer // 2
        if layer % 2 == 0:
            xf = _even_mix(xf, batch, seq, norm_mix[layer], even_w_in[i], mlstm_i_bias[i], mlstm_f_bias[i], mlstm_norm[i],
                           gdn_conv[i], gdn_A_log[i], gdn_dt_bias[i], gdn_norm[i], even_w_out[i])
        else:
            xf = _odd_mix(xf, batch, seq, norm_mix[layer], odd_w_in[i], att_q_norm[i], att_k_norm[i], att_rel_bias[i],
                          ssd_conv[i], ssd_conv_bias[i], ssd_A_log[i], ssd_dt_bias[i], ssd_D[i], ssd_norm[i], odd_w_out[i])
        xf = _moe(xf, norm_ffn[layer], router_group_w[layer], router_group_b[layer], router_expert_w[layer],
                  router_expert_b[layer], moe_w_gate[layer], moe_w_up[layer], moe_w_down[layer])
    return xf.reshape(batch, seq, d)
```

```python
import functools

import jax
import jax.numpy as jnp
from jax import lax
from jax.experimental import pallas as pl
from jax.experimental.pallas import tpu as pltpu

F32 = jnp.float32
BF16 = jnp.bfloat16
I32 = jnp.int32
HIGHEST = lax.Precision.HIGHEST

D_MODEL = 1024
CHUNK = 64
CONV_WIDTH = 4
EPS = 1e-6
LANES = 128

MLSTM_HEADS, MLSTM_DQK, MLSTM_DV = 4, 128, 256
GDN_HEADS, GDN_DK, GDN_DV = 8, 128, 128
ATT_HEADS, ATT_DH, ATT_LEFT_CHUNKS, REL_MAX_PAST = 16, 64, 8, 256
SSD_HEADS, SSD_DH, SSD_STATE, SSD_GROUPS = 32, 64, 128, 4
SSD_HPG = SSD_HEADS // SSD_GROUPS
SSD_INNER = SSD_HEADS * SSD_DH
SSD_BC_W = SSD_GROUPS * SSD_STATE
MOE_GROUPS, MOE_EPG, D_EXPERT = 4, 8, 256

VMEM_LIMIT = 48 * 1024 * 1024


def _cparams(*sem):
    return pltpu.CompilerParams(dimension_semantics=sem, vmem_limit_bytes=VMEM_LIMIT)


def _softplus(x):
    return jnp.maximum(x, 0.0) + jnp.log1p(jnp.exp(-jnp.abs(x)))


def _sigmoid(x):
    return 1.0 / (1.0 + jnp.exp(-x))


def _silu(x):
    return x * _sigmoid(x)


def _rms_rows(x):
    return x * lax.rsqrt(jnp.mean(x * x, axis=-1, keepdims=True) + EPS)


def _dot(a, b):
    return jnp.dot(a, b, preferred_element_type=F32)


def _dot_nt(a, b):
    return lax.dot_general(a, b, (((1,), (1,)), ((), ())), preferred_element_type=F32)


def _dot_tn(a, b):
    return lax.dot_general(a, b, (((0,), (0,)), ((), ())), preferred_element_type=F32)


def _bdot(a, b):
    return lax.dot_general(a, b, (((2,), (1,)), ((0,), (0,))), preferred_element_type=F32)


def _bdot_nt(a, b):
    return lax.dot_general(a, b, (((2,), (2,)), ((0,), (0,))), preferred_element_type=F32)


def _bdot_tn(a, b):
    return lax.dot_general(a, b, (((1,), (1,)), ((0,), (0,))), preferred_element_type=F32)


def _tri(n, strict=False):
    r = lax.broadcasted_iota(I32, (n, n), 0)
    c = lax.broadcasted_iota(I32, (n, n), 1)
    return (r > c) if strict else (r >= c)


def _split_bf16(x):
    hi = x.astype(BF16)
    return hi, (x - hi.astype(F32)).astype(BF16)


def _pair_head_norm(x):
    lo = lax.broadcasted_iota(I32, x.shape, 1) < (LANES // 2)
    sq = x * x
    ms_lo = jnp.sum(jnp.where(lo, sq, 0.0), axis=-1, keepdims=True)
    ms_hi = jnp.sum(jnp.where(lo, 0.0, sq), axis=-1, keepdims=True)
    return x * lax.rsqrt(jnp.where(lo, ms_lo, ms_hi) * (2.0 / LANES) + EPS)


def _norm_matmul_body(x_ref, g_ref, w_ref, wg_ref, post_ref, o_ref, gate_ref, h_scr, *, norm_tiles):
    j = pl.program_id(1)

    @pl.when(j == 0)
    def _():
        hi, lo = _split_bf16(_rms_rows(x_ref[...]) * g_ref[...])
        h_scr[...] = hi
        parts = _dot(hi, wg_ref[...]) + _dot(lo, wg_ref[...])
        gate_ref[...] = parts[:, :LANES] + parts[:, LANES:]

    acc = _dot(h_scr[...], w_ref[...])
    if not norm_tiles:
        o_ref[...] = acc.astype(o_ref.dtype)
        return
    is_norm = functools.reduce(jnp.logical_or, [j == tile for tile in norm_tiles])

    @pl.when(is_norm)
    def _():
        half = LANES // 2
        same_head = (lax.broadcasted_iota(I32, (LANES, LANES), 0) >= half) == (lax.broadcasted_iota(I32, (LANES, LANES), 1) >= half)
        avg = jnp.where(same_head, 1.0 / half, 0.0).astype(BF16)
        for s in range(acc.shape[1] // LANES):
            cols = slice(s * LANES, (s + 1) * LANES)
            blk = acc[:, cols]
            ms = _dot((blk * blk).astype(BF16), avg)
            o_ref[:, cols] = (blk * lax.rsqrt(ms + EPS) * post_ref[:, cols]).astype(o_ref.dtype)

    @pl.when(jnp.logical_not(is_norm))
    def _():
        o_ref[...] = acc.astype(o_ref.dtype)


def _norm_matmul(x, gain, w, w_gate, post=None, norm_tiles=(), tm=1024, tn=1024):
    t, d = x.shape
    n = w.shape[1]
    tm = min(tm, t)
    wg_parts = jnp.concatenate(_split_bf16(w_gate), axis=1)
    if post is None:
        post = jnp.ones((1, n), F32)
    return pl.pallas_call(
        functools.partial(_norm_matmul_body, norm_tiles=tuple(norm_tiles)),
        grid=(t // tm, n // tn),
        in_specs=[
            pl.BlockSpec((tm, d), lambda i, j: (i, 0)),
            pl.BlockSpec((1, d), lambda i, j: (0, 0)),
            pl.BlockSpec((d, tn), lambda i, j: (0, j)),
            pl.BlockSpec((d, 2 * LANES), lambda i, j: (0, 0)),
            pl.BlockSpec((1, tn), lambda i, j: (0, j)),
        ],
        out_specs=[
            pl.BlockSpec((tm, tn), lambda i, j: (i, j)),
            pl.BlockSpec((tm, LANES), lambda i, j: (i, 0)),
        ],
        out_shape=[jax.ShapeDtypeStruct((t, n), BF16), jax.ShapeDtypeStruct((t, LANES), F32)],
        scratch_shapes=[pltpu.VMEM((tm, d), BF16)],
        compiler_params=_cparams("parallel", "arbitrary"),
        name="norm_matmul",
    )(x, gain.reshape(1, d), w, wg_parts, post)


def _gates_body(pre_ref, aux_ref, col_ref, row_ref, *, lanes):
    pre = pre_ref[...] + aux_ref[0:1, :]
    scale = aux_ref[1:2, :]
    lane = lax.broadcasted_iota(I32, pre.shape, 1)
    sp = _softplus(pre)
    vals = jnp.zeros_like(pre)
    cum_mask = jnp.zeros(pre.shape, jnp.bool_)
    for lo, hi, kind, cum in lanes:
        m = (lane >= lo) & (lane < hi)
        if kind == "id":
            v = pre
        elif kind == "logsig":
            v = pre - sp
        elif kind == "softplus":
            v = scale * sp
        else:
            v = _sigmoid(pre)
        vals = jnp.where(m, v, vals)
        if cum:
            cum_mask = cum_mask | m
    pos = lax.broadcasted_iota(I32, pre.shape, 0) & (CHUNK - 1)
    c = vals
    k = 1
    while k < CHUNK:
        c = c + jnp.where(pos >= k, pltpu.roll(c, k, 0), 0.0)
        k *= 2
    out = jnp.where(cum_mask, c, vals)
    col_ref[...] = out
    row_ref[...] = out.T


def _gates(pre, aux, lanes, tm=2048):
    t = pre.shape[0]
    tm = min(tm, t)
    return pl.pallas_call(
        functools.partial(_gates_body, lanes=lanes),
        grid=(t // tm,),
        in_specs=[
            pl.BlockSpec((tm, LANES), lambda i: (i, 0)),
            pl.BlockSpec((8, LANES), lambda i: (0, 0)),
        ],
        out_specs=[
            pl.BlockSpec((tm, LANES), lambda i: (i, 0)),
            pl.BlockSpec((LANES, tm), lambda i: (0, i)),
        ],
        out_shape=[jax.ShapeDtypeStruct((t, LANES), F32), jax.ShapeDtypeStruct((LANES, t), F32)],
        compiler_params=_cparams("parallel"),
        name="gates",
    )(pre, aux)


def _mlstm_body(q_ref, k_ref, v_ref, og_ref, gcol_ref, grow_ref, mn_ref, o_ref, c_scr, n_scr, m_scr, *, ts):
    @pl.when(pl.program_id(1) == 0)
    def _():
        c_scr[...] = jnp.zeros_like(c_scr)
        n_scr[...] = jnp.zeros_like(n_scr)
        m_scr[...] = jnp.zeros_like(m_scr)

    scale = MLSTM_DQK ** -0.5
    nh, nc = MLSTM_HEADS, ts // CHUNK
    nb = nh * nc
    q = jnp.stack([q_ref[:, h * MLSTM_DQK:(h + 1) * MLSTM_DQK] for h in range(nh)], axis=0).reshape(nb, CHUNK, MLSTM_DQK)
    k = jnp.stack([k_ref[:, h * MLSTM_DQK:(h + 1) * MLSTM_DQK] for h in range(nh)], axis=0).reshape(nb, CHUNK, MLSTM_DQK)
    v = jnp.stack([v_ref[:, h * MLSTM_DV:(h + 1) * MLSTM_DV] for h in range(nh)], axis=0).reshape(nb, CHUNK, MLSTM_DV)
    gcol = gcol_ref[...]
    grow = grow_ref[...]
    li_col = jnp.stack([gcol[:, h:h + 1] for h in range(nh)], axis=0).reshape(nb, CHUNK, 1)
    b_col = jnp.stack([gcol[:, nh + h:nh + h + 1] for h in range(nh)], axis=0).reshape(nb, CHUNK, 1)
    li_row = jnp.stack([grow[0:nh, c * CHUNK:(c + 1) * CHUNK] for c in range(nc)], axis=1).reshape(nb, 1, CHUNK)
    b_row = jnp.stack([grow[nh:2 * nh, c * CHUNK:(c + 1) * CHUNK] for c in range(nc)], axis=1).reshape(nb, 1, CHUNK)

    b_end = b_col[:, CHUNK - 1:CHUNK, :]
    g_end = b_end - b_col + li_col
    m_loc = jnp.max(g_end, axis=1, keepdims=True)
    kw = k.astype(F32) * jnp.exp(g_end - m_loc)
    upd_c = _bdot_tn(kw.astype(BF16), v).reshape(nh, nc, MLSTM_DQK, MLSTM_DV)
    upd_n = jnp.sum(kw, axis=1, keepdims=True).reshape(nh, nc, 1, MLSTM_DQK)
    log_d = jnp.where(_tri(CHUNK)[None], b_col - b_row + li_row, -jnp.inf)
    max_d = jnp.max(log_d, axis=2, keepdims=True)
    s_raw = _bdot_nt(q, k)

    b_end4 = b_end.reshape(nh, nc, 1, 1)
    m_loc4 = m_loc.reshape(nh, nc, 1, 1)
    m_prev = m_scr[...][:, 0:1, 0:1]
    m_prevs, s_olds, s_upds = [], [], []
    for c in range(nc):
        m_prevs.append(m_prev)
        m_new = jnp.maximum(b_end4[:, c] + m_prev, m_loc4[:, c])
        s_olds.append(jnp.exp(b_end4[:, c] + m_prev - m_new))
        s_upds.append(jnp.exp(m_loc4[:, c] - m_new))
        m_prev = m_new
    m_scr[...] = jnp.broadcast_to(m_prev, m_scr.shape)

    inter_log = b_col + jnp.stack(m_prevs, axis=1).reshape(nb, 1, 1)
    m_t = jnp.maximum(inter_log, max_d)
    w_inter = jnp.exp(inter_log - m_t)
    scores = s_raw * (scale * jnp.exp(log_d - m_t))
    num = _bdot(scores.astype(BF16), v)
    den = jnp.sum(scores, axis=2, keepdims=True)

    q4 = q.reshape(nh, nc, CHUNK, MLSTM_DQK)
    c_state = c_scr[...]
    n_state = n_scr[...][:, 0:1, :]
    qcs, qns = [], []
    for c in range(nc):
        qcs.append(_bdot(q4[:, c], c_state.astype(BF16)))
        qns.append(jnp.sum(q4[:, c].astype(F32) * n_state, axis=2, keepdims=True))
        c_state = s_olds[c] * c_state + s_upds[c] * upd_c[:, c]
        n_state = s_olds[c] * n_state + s_upds[c] * upd_n[:, c]
    c_scr[...] = c_state
    n_scr[...] = jnp.broadcast_to(n_state, n_scr.shape)

    num = num + w_inter * (jnp.stack(qcs, axis=1).reshape(nb, CHUNK, MLSTM_DV) * scale)
    den = den + w_inter * (jnp.stack(qns, axis=1).reshape(nb, CHUNK, 1) * scale)
    hout = num / jnp.maximum(jnp.abs(den), jnp.exp(-m_t))
    hn = _rms_rows(hout).reshape(nh, ts, MLSTM_DV)
    for h in range(nh):
        cols = slice(h * MLSTM_DV, (h + 1) * MLSTM_DV)
        o_ref[:, cols] = (hn[h] * mn_ref[:, cols] * _sigmoid(og_ref[:, cols].astype(F32))).astype(o_ref.dtype)


def _mlstm(p, gcol, grow, m_norm, batch, seq, ts=512):
    t = batch * seq
    ts = min(ts, seq)
    nt = seq // ts
    qk_w = MLSTM_HEADS * MLSTM_DQK
    v_w = MLSTM_HEADS * MLSTM_DV
    row = lambda b, i: b * nt + i
    return pl.pallas_call(
        functools.partial(_mlstm_body, ts=ts),
        grid=(batch, nt),
        in_specs=[
            pl.BlockSpec((ts, qk_w), lambda b, i: (row(b, i), 0)),
            pl.BlockSpec((ts, qk_w), lambda b, i: (row(b, i), 1)),
            pl.BlockSpec((ts, v_w), lambda b, i: (row(b, i), 1)),
            pl.BlockSpec((ts, v_w), lambda b, i: (row(b, i), 2)),
            pl.BlockSpec((ts, LANES), lambda b, i: (row(b, i), 0)),
            pl.BlockSpec((8, ts), lambda b, i: (0, row(b, i))),
            pl.BlockSpec((1, v_w), lambda b, i: (0, 0)),
        ],
        out_specs=pl.BlockSpec((ts, v_w), lambda b, i: (row(b, i), 0)),
        out_shape=jax.ShapeDtypeStruct((t, v_w), BF16),
        scratch_shapes=[
            pltpu.VMEM((MLSTM_HEADS, MLSTM_DQK, MLSTM_DV), F32),
            pltpu.VMEM((MLSTM_HEADS, 8, MLSTM_DQK), F32),
            pltpu.VMEM((MLSTM_HEADS, 8, LANES), F32),
        ],
        compiler_params=_cparams("parallel", "arbitrary"),
        name="mlstm",
    )(p, p, p, p, gcol, grow, m_norm.reshape(1, v_w))


def _causal_conv(x_ref, tail_ref, w_ref, cols, ts, bias=None):
    xp = jnp.concatenate([tail_ref[:, cols], x_ref[:, cols].astype(F32)], axis=0)
    y = w_ref[CONV_WIDTH - 1:CONV_WIDTH, cols] * xp[8:, :]
    for j in range(CONV_WIDTH - 1):
        y = y + w_ref[j:j + 1, cols] * pltpu.roll(xp, CONV_WIDTH - 1 - j, 0)[8:, :]
    if bias is not None:
        y = y + bias[:, cols]
    return y


def _gdn_body(x_ref, z_ref, gcol_ref, grow_ref, cw_ref, gn_ref, o_ref, tail_scr, s_scr, *, ts):
    @pl.when(pl.program_id(1) == 0)
    def _():
        tail_scr[...] = jnp.zeros_like(tail_scr)
        s_scr[...] = jnp.zeros_like(s_scr)

    nh, nc = GDN_HEADS, ts // CHUNK
    nb = nh * nc
    qk_w = nh * GDN_DK

    def conv_heads(base):
        return jnp.stack([_silu(_causal_conv(x_ref, tail_scr, cw_ref, slice(base + h * GDN_DK, base + (h + 1) * GDN_DK), ts))
                          for h in range(nh)], axis=0)

    q = conv_heads(0)
    k = conv_heads(qk_w)
    v = conv_heads(2 * qk_w).reshape(nb, CHUNK, GDN_DV)
    q = (q * lax.rsqrt(jnp.sum(q * q, axis=-1, keepdims=True) + EPS) * (GDN_DK ** -0.5)).reshape(nb, CHUNK, GDN_DK)
    k = (k * lax.rsqrt(jnp.sum(k * k, axis=-1, keepdims=True) + EPS)).reshape(nb, CHUNK, GDN_DK)

    gcol = gcol_ref[...]
    grow = grow_ref[...]
    g_col = jnp.stack([gcol[:, 16 + h:17 + h] for h in range(nh)], axis=0).reshape(nb, CHUNK, 1)
    beta = jnp.stack([gcol[:, 24 + h:25 + h] for h in range(nh)], axis=0).reshape(nb, CHUNK, 1)
    g_row = jnp.stack([grow[:, c * CHUNK:(c + 1) * CHUNK] for c in range(nc)], axis=1).reshape(nb, 1, CHUNK)
    g_end = g_col[:, CHUNK - 1:CHUNK, :]

    tri = _tri(CHUNK)[None]
    tri_strict = _tri(CHUNK, strict=True)[None]
    eye = (lax.broadcasted_iota(I32, (CHUNK, CHUNK), 0) == lax.broadcasted_iota(I32, (CHUNK, CHUNK), 1)).astype(F32)[None]
    kb = k.astype(BF16)
    decay = jnp.exp(jnp.where(tri, g_col - g_row, -jnp.inf))
    qk_kk = _bdot_nt(jnp.concatenate([q, k], axis=1).astype(BF16), kb)
    a_mat = jnp.where(tri_strict, beta * qk_kk[:, CHUNK:] * decay, 0.0)
    p = -a_mat
    y = eye + p
    pb = p.astype(BF16)
    p = _bdot(pb, pb)
    for _ in range(4):
        py = _bdot(p.astype(BF16), jnp.concatenate([y, p], axis=2).astype(BF16))
        y = y + py[:, :, :CHUNK]
        p = py[:, :, CHUNK:]
    y = y + _bdot(p.astype(BF16), y.astype(BF16))
    eg = jnp.exp(g_col)
    rhs = jnp.concatenate([beta * v, (beta * eg) * k], axis=2).astype(BF16)
    sol = _bdot(y.astype(BF16), rhs)
    u = sol[:, :, :GDN_DV].reshape(nh, nc, CHUNK, GDN_DV)
    wq = jnp.concatenate([sol[:, :, GDN_DV:], q * eg], axis=1).astype(BF16).reshape(nh, nc, 2 * CHUNK, GDN_DK)
    qk = (qk_kk[:, :CHUNK] * decay).astype(BF16).reshape(nh, nc, CHUNK, CHUNK)
    k_dec = (k * jnp.exp(g_end - g_col)).astype(BF16).reshape(nh, nc, CHUNK, GDN_DK)
    g_tot = jnp.exp(g_end).reshape(nh, nc, 1, 1)

    s = s_scr[...]
    outs = []
    for c in range(nc):
        ws = _bdot(wq[:, c], s.astype(BF16))
        vb = (u[:, c] - ws[:, :CHUNK]).astype(BF16)
        outs.append(ws[:, CHUNK:] + _bdot(qk[:, c], vb))
        s = g_tot[:, c] * s + _bdot_tn(k_dec[:, c], vb)
    s_scr[...] = s

    o = jnp.stack(outs, axis=1).reshape(nh, ts, GDN_DV)
    on = _rms_rows(o) * gn_ref[...]
    for h in range(nh):
        cols = slice(h * GDN_DV, (h + 1) * GDN_DV)
        o_ref[:, cols] = (on[h] * _silu(z_ref[:, cols].astype(F32))).astype(o_ref.dtype)
    tail_scr[...] = x_ref[ts - 8:ts, :].astype(F32)


def _gdn(p, gcol, grow, conv_w, g_norm, batch, seq, ts=256):
    t = batch * seq
    ts = min(ts, seq)
    nt = seq // ts
    qkv_w = 3 * GDN_HEADS * GDN_DK
    v_w = GDN_HEADS * GDN_DV
    row = lambda b, i: b * nt + i
    return pl.pallas_call(
        functools.partial(_gdn_body, ts=ts),
        grid=(batch, nt),
        in_specs=[
            pl.BlockSpec((ts, qkv_w), lambda b, i: (row(b, i), 1)),
            pl.BlockSpec((ts, v_w), lambda b, i: (row(b, i), 6)),
            pl.BlockSpec((ts, LANES), lambda b, i: (row(b, i), 0)),
            pl.BlockSpec((8, ts), lambda b, i: (2, row(b, i))),
            pl.BlockSpec((CONV_WIDTH, qkv_w), lambda b, i: (0, 0)),
            pl.BlockSpec((1, GDN_DV), lambda b, i: (0, 0)),
        ],
        out_specs=pl.BlockSpec((ts, v_w), lambda b, i: (row(b, i), 0)),
        out_shape=jax.ShapeDtypeStruct((t, v_w), BF16),
        scratch_shapes=[
            pltpu.VMEM((8, qkv_w), F32),
            pltpu.VMEM((GDN_HEADS, GDN_DK, GDN_DV), F32),
        ],
        compiler_params=_cparams("parallel", "arbitrary"),
        name="gdn",
    )(p, p, gcol, grow, conv_w, g_norm.reshape(1, GDN_DV))


def _out_proj_body(x_ref, a_ref, b_ref, wa_ref, wb_ref, o_ref):
    o_ref[...] = x_ref[...] + _dot(a_ref[...], wa_ref[...]) + _dot(b_ref[...], wb_ref[...])


def _out_proj(x, a, b, wa, wb, tm=512):
    t, d = x.shape
    tm = min(tm, t)
    ka, kb = a.shape[1], b.shape[1]
    return pl.pallas_call(
        _out_proj_body,
        grid=(t // tm,),
        in_specs=[
            pl.BlockSpec((tm, d), lambda i: (i, 0)),
            pl.BlockSpec((tm, ka), lambda i: (i, 0)),
            pl.BlockSpec((tm, kb), lambda i: (i, 0)),
            pl.BlockSpec((ka, d), lambda i: (0, 0)),
            pl.BlockSpec((kb, d), lambda i: (0, 0)),
        ],
        out_specs=pl.BlockSpec((tm, d), lambda i: (i, 0)),
        out_shape=jax.ShapeDtypeStruct((t, d), F32),
        compiler_params=_cparams("parallel"),
        name="out_proj",
    )(x, a, b, wa, wb)


def _lane_table(entries):
    row = jnp.zeros((LANES,), F32)
    for start, vals in entries:
        row = lax.dynamic_update_slice(row, vals.astype(F32).reshape(-1), (start,))
    return row.reshape(1, LANES)


def _gate_weight(w_in, entries):
    w = jnp.zeros((w_in.shape[0], LANES), F32)
    for start, lo, hi in entries:
        w = lax.dynamic_update_slice(w, w_in[:, lo:hi].astype(F32), (0, start))
    return w


EVEN_GATE_LANES = ((0, 4, "id", False), (4, 8, "logsig", True), (16, 24, "softplus", True), (24, 32, "sigmoid", False))


def _even_mix(xf, batch, seq, gain, w_in, i_bias, f_bias, m_norm, conv_w, a_log, dt_bias, g_norm, w_out):
    w_big = jnp.concatenate([w_in[:, 0:3072], w_in[:, 3080:7176]], axis=1).astype(BF16)
    w_gate = _gate_weight(w_in, ((0, 3072, 3076), (4, 3076, 3080), (16, 7176, 7184), (24, 7184, 7192)))
    aux = jnp.concatenate([
        _lane_table(((0, i_bias), (4, f_bias), (16, dt_bias))),
        _lane_table(((16, -jnp.exp(a_log)),)),
        jnp.zeros((6, LANES), F32)], axis=0)
    p, pre = _norm_matmul(xf, gain, w_big, w_gate, tn=w_big.shape[1] // 4)
    gcol, grow = _gates(pre, aux, EVEN_GATE_LANES)
    hm = _mlstm(p, gcol, grow, m_norm, batch, seq)
    o = _gdn(p, gcol, grow, conv_w, g_norm, batch, seq)
    v_w = MLSTM_HEADS * MLSTM_DV
    return _out_proj(xf, hm, o, w_out[:v_w].astype(BF16), w_out[v_w:].astype(BF16))


ATT_TQ = 256


def _attn_body(q_ref, k0_ref, k1_ref, k2_ref, v0_ref, v1_ref, v2_ref, bias_ref, o_ref):
    i = pl.program_id(1)
    tq = q_ref.shape[0]
    lo_half = lax.broadcasted_iota(I32, (1, LANES), 1) < ATT_DH
    lo_rows = lax.broadcasted_iota(I32, (LANES, 1), 0) < ATT_DH
    k_refs = (k0_ref, k1_ref, k2_ref)
    v_refs = (v0_ref, v1_ref, v2_ref)
    for hp in range(ATT_HEADS // 2):
        cols = slice(hp * LANES, (hp + 1) * LANES)
        q = q_ref[:, cols]
        ks = [r[:, cols] for r in k_refs]
        vs = [r[:, cols] for r in v_refs]
        outs = []
        for hh in range(2):
            own = lo_half if hh == 0 else jnp.logical_not(lo_half)
            qm = jnp.where(own, q, jnp.zeros_like(q))
            s = [_dot_nt(ks[j], qm) + bias_ref[2 * hp + hh, j * tq:(j + 1) * tq, :] for j in range(3)]
            s[0] = jnp.where(i >= 2, s[0], -jnp.inf)
            s[1] = jnp.where(i >= 1, s[1], -jnp.inf)
            m = functools.reduce(jnp.maximum, [jnp.max(sj, axis=0, keepdims=True) for sj in s])
            o = functools.reduce(jnp.add, [_dot_tn(jnp.where(own, vs[j], jnp.ones_like(vs[j])),
                                                   jnp.exp((s[j] - m).astype(BF16))) for j in range(3)])
            outs.append(o / pltpu.roll(o, ATT_DH, 0))
        o_ref[:, cols] = jnp.where(lo_rows, outs[0], outs[1]).T.astype(o_ref.dtype)


def _attn(p, bias, batch, seq, col0):
    t = batch * seq
    tq = ATT_TQ
    nt = seq // tq
    w = ATT_HEADS * ATT_DH
    row = lambda b, i, back: b * nt + jnp.maximum(i - back, 0)
    kv_spec = lambda off, back: pl.BlockSpec((tq, w), lambda b, i: (row(b, i, back), col0 + off))
    return pl.pallas_call(
        _attn_body,
        grid=(batch, nt),
        in_specs=[
            pl.BlockSpec((tq, w), lambda b, i: (row(b, i, 0), col0)),
            kv_spec(1, 2), kv_spec(1, 1), kv_spec(1, 0),
            kv_spec(2, 2), kv_spec(2, 1), kv_spec(2, 0),
            pl.BlockSpec((ATT_HEADS, 3 * tq, tq), lambda b, i: (0, 0, 0)),
        ],
        out_specs=pl.BlockSpec((tq, w), lambda b, i: (row(b, i, 0), 0)),
        out_shape=jax.ShapeDtypeStruct((t, w), BF16),
        compiler_params=_cparams("parallel", "arbitrary"),
        name="band_attn",
    )(p, p, p, p, p, p, p, bias)


def _attn_bias_table(rel_bias):
    pad = ATT_LEFT_CHUNKS * CHUNK
    band = pad + CHUNK
    rel = jnp.arange(CHUNK)[:, None] + pad - jnp.arange(band)[None, :]
    rel_idx = jnp.clip(rel, -(CHUNK - 1), REL_MAX_PAST) + (CHUNK - 1)
    onehot = (rel_idx[None, :, :] == jnp.arange(rel_bias.shape[1])[:, None, None]).astype(F32)
    bias64 = jnp.einsum("hr,rlj->hlj", rel_bias.astype(F32), onehot, precision=HIGHEST)
    nq = ATT_TQ // CHUNK
    rows = [jnp.pad(bias64, ((0, 0), (0, 0), (a * CHUNK, (nq - 1 - a) * CHUNK)), constant_values=-jnp.inf)
            for a in range(nq)]
    return jnp.swapaxes(jnp.concatenate(rows, axis=1), 1, 2)


def _ssd_body(x_ref, z_ref, gcol_ref, grow_ref, cw_ref, cb_ref, dsk_ref, sn_ref, o_ref,
              tail_scr, xc_scr, y_scr, s_scr, *, ts):
    @pl.when(pl.program_id(1) == 0)
    def _():
        tail_scr[...] = jnp.zeros_like(tail_scr)
        s_scr[...] = jnp.zeros_like(s_scr)

    conv_w = x_ref.shape[1]
    for j in range(conv_w // LANES):
        cols = slice(j * LANES, (j + 1) * LANES)
        xc_scr[:, cols] = _silu(_causal_conv(x_ref, tail_scr, cw_ref, cols, ts, bias=cb_ref))

    nc, ng, npair = ts // CHUNK, SSD_GROUPS, SSD_HEADS // 2
    ppg = npair // ng
    nb = npair * nc
    gw = SSD_HPG * SSD_DH
    lo_half = lax.broadcasted_iota(I32, (1, LANES), 1) < SSD_DH
    gcol = gcol_ref[...]
    grow = grow_ref[...]

    def pair_cols(base):
        return jnp.stack([jnp.where(lo_half, gcol[:, base + p:base + p + 1], gcol[:, base + npair + p:base + npair + p + 1])
                          for p in range(npair)], axis=0).reshape(nb, CHUNK, LANES)

    dt = pair_cols(0)
    a_col = pair_cols(SSD_HEADS)
    a_row = jnp.stack([jnp.concatenate([grow[0:npair, c * CHUNK:(c + 1) * CHUNK], grow[npair:, c * CHUNK:(c + 1) * CHUNK]], axis=1)
                       for c in range(nc)], axis=1).reshape(nb, 1, LANES)
    a_end = a_col[:, CHUNK - 1:CHUNK, :]
    xs = jnp.stack([xc_scr[:, p * LANES:(p + 1) * LANES] for p in range(npair)], axis=0).reshape(nb, CHUNK, LANES)
    dsk = jnp.stack([dsk_ref[:, p * LANES:(p + 1) * LANES] for p in range(npair)], axis=0)
    dsk = jnp.broadcast_to(dsk[:, None], (npair, nc, 1, LANES)).reshape(nb, 1, LANES)
    bm = jnp.stack([xc_scr[:, SSD_INNER + g * SSD_STATE:SSD_INNER + (g + 1) * SSD_STATE] for g in range(ng)], axis=0)
    cm = jnp.stack([xc_scr[:, SSD_INNER + SSD_BC_W + g * SSD_STATE:SSD_INNER + SSD_BC_W + (g + 1) * SSD_STATE]
                    for g in range(ng)], axis=0)
    bm = bm.astype(BF16).reshape(ng * nc, CHUNK, SSD_STATE)
    cm = cm.astype(BF16).reshape(ng * nc, CHUNK, SSD_STATE)

    tri2 = (lax.broadcasted_iota(I32, (CHUNK, LANES), 0) >= (lax.broadcasted_iota(I32, (CHUNK, LANES), 1) & (SSD_DH - 1)))[None]
    decay = jnp.exp(jnp.where(tri2, a_col - a_row, -jnp.inf))
    cb = _bdot_nt(cm, bm)
    cb2 = jnp.concatenate([cb, cb], axis=2).reshape(ng, 1, nc, CHUNK, LANES)
    cb2 = jnp.broadcast_to(cb2, (ng, ppg, nc, CHUNK, LANES)).reshape(nb, CHUNK, LANES)
    xd = xs * dt
    rhs = jnp.concatenate([jnp.where(lo_half, xd, 0.0), jnp.where(lo_half, 0.0, xd)], axis=1).astype(BF16)
    y_in = (_bdot((cb2 * decay).astype(BF16), rhs) + dsk * xs).reshape(ng, ppg, nc, CHUNK, LANES)
    ea = jnp.exp(a_col).reshape(ng, ppg, nc, CHUNK, LANES)
    dxd = (xd * jnp.exp(a_end - a_col)).astype(BF16).reshape(ng, ppg, nc, CHUNK, LANES)
    cdec = jnp.exp(a_end).reshape(ng, ppg, nc, 1, LANES)
    bm = bm.reshape(ng, nc, CHUNK, SSD_STATE)
    cm = cm.reshape(ng, nc, CHUNK, SSD_STATE)

    s = s_scr[...]
    for c in range(nc):
        rows = slice(c * CHUNK, (c + 1) * CHUNK)
        y_off = _bdot(cm[:, c], s.astype(BF16))
        for j in range(ppg):
            yj = y_in[:, j, c] + y_off[:, :, j * LANES:(j + 1) * LANES] * ea[:, j, c]
            for g in range(ng):
                y_scr[rows, (g * ppg + j) * LANES:(g * ppg + j + 1) * LANES] = yj[g]
        dx = jnp.concatenate([dxd[:, j, c] for j in range(ppg)], axis=2)
        cd = jnp.concatenate([cdec[:, j, c] for j in range(ppg)], axis=2)
        s = s * cd + _bdot_tn(bm[:, c], dx)
    s_scr[...] = s

    for g in range(SSD_GROUPS):
        cols = slice(g * gw, (g + 1) * gw)
        yg = y_scr[:, cols] * _silu(z_ref[:, cols].astype(F32))
        o_ref[:, cols] = (_rms_rows(yg) * sn_ref[:, cols]).astype(o_ref.dtype)
    tail_scr[...] = x_ref[ts - 8:ts, :].astype(F32)


def _ssd(p, gcol, grow, conv_w, conv_b, d_skip, s_norm, batch, seq, ts=256):
    t = batch * seq
    ts = min(ts, seq)
    nt = seq // ts
    cw = SSD_INNER + 2 * SSD_BC_W
    row = lambda b, i: b * nt + i
    return pl.pallas_call(
        functools.partial(_ssd_body, ts=ts),
        grid=(batch, nt),
        in_specs=[
            pl.BlockSpec((ts, cw), lambda b, i: (row(b, i), 0)),
            pl.BlockSpec((ts, SSD_INNER), lambda b, i: (row(b, i), 3)),
            pl.BlockSpec((ts, LANES), lambda b, i: (row(b, i), 0)),
            pl.BlockSpec((SSD_HEADS, ts), lambda b, i: (1, row(b, i))),
            pl.BlockSpec((CONV_WIDTH, cw), lambda b, i: (0, 0)),
            pl.BlockSpec((1, cw), lambda b, i: (0, 0)),
            pl.BlockSpec((1, SSD_INNER), lambda b, i: (0, 0)),
            pl.BlockSpec((1, SSD_INNER), lambda b, i: (0, 0)),
        ],
        out_specs=pl.BlockSpec((ts, SSD_INNER), lambda b, i: (row(b, i), 0)),
        out_shape=jax.ShapeDtypeStruct((t, SSD_INNER), BF16),
        scratch_shapes=[
            pltpu.VMEM((8, cw), F32),
            pltpu.VMEM((ts, cw), F32),
            pltpu.VMEM((ts, SSD_INNER), F32),
            pltpu.VMEM((SSD_GROUPS, SSD_STATE, SSD_HPG * SSD_DH), F32),
        ],
        compiler_params=_cparams("parallel", "arbitrary"),
        name="ssd",
    )(p, p, gcol, grow, conv_w, conv_b.reshape(1, cw), jnp.repeat(d_skip, SSD_DH).reshape(1, SSD_INNER),
      s_norm.reshape(1, SSD_INNER))


ODD_GATE_LANES = ((0, 32, "softplus", False), (32, 64, "softplus", True))


def _odd_mix(xf, batch, seq, gain, w_in, q_gain, k_gain, rel_bias, conv_w, conv_b, a_log, dt_bias, d_skip, s_norm, w_out):
    w_big = jnp.concatenate([w_in[:, 5120:8192], w_in[:, 0:3072], w_in[:, 3072:5120]], axis=1).astype(BF16)
    even_odd = lambda a: jnp.concatenate([a[..., 0::2], a[..., 1::2]], axis=-1)
    w_dt = even_odd(w_in[:, 8192:8224])
    w_gate = _gate_weight(jnp.concatenate([w_dt, w_dt], axis=1), ((0, 0, 2 * SSD_HEADS),))
    aux = jnp.concatenate([
        _lane_table(((0, even_odd(dt_bias)), (SSD_HEADS, even_odd(dt_bias)))),
        _lane_table(((0, jnp.ones((SSD_HEADS,), F32)), (SSD_HEADS, -jnp.exp(even_odd(a_log))))),
        jnp.zeros((6, LANES), F32)], axis=0)
    a_w = ATT_HEADS * ATT_DH
    post = jnp.ones((1, w_big.shape[1]), F32)
    post = lax.dynamic_update_slice(post, jnp.tile(q_gain.astype(F32) * ATT_DH ** -0.5, ATT_HEADS).reshape(1, a_w), (0, 3072))
    post = lax.dynamic_update_slice(post, jnp.tile(k_gain.astype(F32), ATT_HEADS).reshape(1, a_w), (0, 3072 + a_w))
    p, pre = _norm_matmul(xf, gain, w_big, w_gate, post=post, norm_tiles=(3, 4))
    gcol, grow = _gates(pre, aux, ODD_GATE_LANES)
    attn = _attn(p, _attn_bias_table(rel_bias), batch, seq, col0=3072 // a_w)
    y = _ssd(p, gcol, grow, conv_w, conv_b, d_skip, s_norm, batch, seq)
    return _out_proj(xf, attn, y, w_out[:a_w].astype(BF16), w_out[a_w:].astype(BF16))


ROW_W = D_MODEL + LANES
MOE_TM = 512
DMA_ROWS = 512


def _router_body(x_ref, g_ref, w_ref, b_ref, rows_ref, info_ref, cnt_ref, carry_scr):
    @pl.when(pl.program_id(0) == 0)
    def _():
        carry_scr[...] = jnp.zeros_like(carry_scr)

    h = _rms_rows(x_ref[...]) * g_ref[...]
    h_hi, h_lo = _split_bf16(h)
    parts = _dot(h_hi, w_ref[...]) + _dot(h_lo, w_ref[...])
    logits = parts[:, :LANES] + parts[:, LANES:] + b_ref[...]
    tm = logits.shape[0]
    n_log = MOE_GROUPS + MOE_GROUPS * MOE_EPG + 4
    lt = logits.T[0:n_log, :]
    row = lax.broadcasted_iota(I32, lt.shape, 0).astype(F32)
    ninf = -jnp.inf

    def first_max(v):
        m = jnp.max(v, axis=0, keepdims=True)
        return m, jnp.min(jnp.where(v == m, row, float(LANES)), axis=0, keepdims=True)

    glog = jnp.where(row < MOE_GROUPS, lt, ninf)
    gmax, gidx = first_max(glog)
    grp_w = 1.0 / jnp.sum(jnp.exp(glog - gmax), axis=0, keepdims=True)
    e_lo = MOE_GROUPS + gidx * MOE_EPG
    elog = jnp.where((row >= e_lo) & (row < e_lo + MOE_EPG), lt, ninf)
    v1, i1 = first_max(elog)
    v2, i2 = first_max(jnp.where(row == i1, ninf, elog))
    e21 = jnp.exp(v2 - v1)
    w1 = 1.0 / (1.0 + e21)
    comb = grp_w * (jnp.where(row == i1, w1, 0.0) + jnp.where(row == i2, e21 * w1, 0.0))
    local = functools.reduce(jnp.add, [comb[MOE_GROUPS + g * MOE_EPG:MOE_GROUPS + (g + 1) * MOE_EPG, :]
                                       for g in range(MOE_GROUPS)])
    local = jnp.concatenate([local, jnp.zeros((LANES - MOE_EPG, tm), F32)], axis=0).T
    rows_ref[...] = jnp.concatenate([h, local], axis=1)

    row8 = lax.broadcasted_iota(I32, (8, tm), 0).astype(F32)
    onehot = jnp.where(row8 == gidx, 1.0, 0.0)
    earlier =jnp.where(lax.broadcasted_iota(I32, (tm, tm), 0) < lax.broadcasted_iota(I32, (tm, tm), 1), 1.0, 0.0).astype(BF16)
    carry = carry_scr[:, 0:1]
    before = _dot(onehot.astype(BF16), earlier) + carry
    rank = jnp.sum(onehot * before, axis=0, keepdims=True)
    carry = carry + jnp.sum(onehot, axis=1, keepdims=True)
    carry_scr[...] = jnp.broadcast_to(carry, carry_scr.shape)
    cnt_ref[...] = jnp.broadcast_to(carry, cnt_ref.shape)
    info_ref[...] = jnp.where(row8 == 0, gidx, jnp.where(row8 == 1, rank, 0.0))


def _router(x, gain, w, b, tm=512):
    t, d = x.shape
    tm = min(tm, t)
    w_parts = jnp.concatenate(_split_bf16(w), axis=1)
    return pl.pallas_call(
        _router_body,
        grid=(t // tm,),
        in_specs=[
            pl.BlockSpec((tm, d), lambda i: (i, 0)),
            pl.BlockSpec((1, d), lambda i: (0, 0)),
            pl.BlockSpec((d, 2 * LANES), lambda i: (0, 0)),
            pl.BlockSpec((1, LANES), lambda i: (0, 0)),
        ],
        out_specs=[
            pl.BlockSpec((tm, ROW_W), lambda i: (i, 0)),
            pl.BlockSpec((8, tm), lambda i: (0, i)),
            pl.BlockSpec((8, LANES), lambda i: (0, 0)),
        ],
        out_shape=[
            jax.ShapeDtypeStruct((t, ROW_W), F32),
            jax.ShapeDtypeStruct((8, t), F32),
            jax.ShapeDtypeStruct((8, LANES), F32),
        ],
        scratch_shapes=[pltpu.VMEM((8, LANES), F32)],
        compiler_params=_cparams("arbitrary"),
        name="moe_router",
    )(x, gain.reshape(1, d), w_parts, b)


def _dispatch_body(pos_ref, pad_ref, rows_ref, out_ref, zero_scr, sem, zero_sem):
    i = pl.program_id(0)
    base = i * DMA_ROWS

    @pl.when(i == 0)
    def _():
        zero_scr[...] = jnp.zeros_like(zero_scr)
        n_tiles = out_ref.shape[0] // MOE_TM
        fills = []
        for g in range(MOE_GROUPS):
            first, length = pad_ref[g], pad_ref[MOE_GROUPS + g]
            end = first + length
            for r in range(7):
                fills.append((r < jnp.bitwise_and(length, 7), first + r, 1))
            piece = MOE_TM // 2
            while piece >= 8:
                fills.append((jnp.bitwise_and(length, piece) != 0,
                              pl.multiple_of(end - jnp.bitwise_and(length, -piece), 8), piece))
                piece //= 2
        for j in range(MOE_GROUPS):
            tile = pad_ref[2 * MOE_GROUPS] + j
            fills.append((tile < n_tiles, pl.multiple_of(tile * MOE_TM, 8), MOE_TM))

        def fill_copy(row, n):
            return pltpu.make_async_copy(zero_scr.at[pl.ds(0, n)], out_ref.at[pl.ds(row, n)], zero_sem)

        for cond, row, n in fills:
            pl.when(cond)(lambda row=row, n=n: fill_copy(row, n).start())
        for cond, row, n in fills:
            pl.when(cond)(lambda row=row, n=n: fill_copy(row, n).wait())

    def row_copy(r):
        return pltpu.make_async_copy(rows_ref.at[pl.ds(r, 1)], out_ref.at[pl.ds(pos_ref[base + r], 1)], sem)

    def start(g, carry):
        for k in range(8):
            row_copy(g * 8 + k).start(priority=k % 2)
        return carry

    lax.fori_loop(0, DMA_ROWS // 8, start, 0)
    pltpu.make_async_copy(rows_ref, out_ref.at[pl.ds(0, DMA_ROWS)], sem).wait()


def _dispatch(pos, pad, rows, n_sorted):
    t, w = rows.shape
    return pl.pallas_call(
        _dispatch_body,
        grid_spec=pltpu.PrefetchScalarGridSpec(
            num_scalar_prefetch=2,
            grid=(t // DMA_ROWS,),
            in_specs=[pl.BlockSpec((DMA_ROWS, w), lambda i, pos, pad: (i, 0))],
            out_specs=pl.BlockSpec(memory_space=pl.ANY),
            scratch_shapes=[pltpu.VMEM((MOE_TM, w), rows.dtype), pltpu.SemaphoreType.DMA(()), pltpu.SemaphoreType.DMA(())],
        ),
        out_shape=jax.ShapeDtypeStruct((n_sorted, w), rows.dtype),
        compiler_params=_cparams("arbitrary"),
        name="moe_dispatch",
    )(pos, pad, rows)


def _experts_body(tile_group_ref, n_used_ref, xs_ref, wg_ref, wu_ref, wd_ref, y_ref, act_scr):
    del tile_group_ref
    i = pl.program_id(0)

    @pl.when(i < n_used_ref[0])
    def _():
        x = xs_ref[:, :D_MODEL].astype(BF16)
        comb = xs_ref[:, D_MODEL:]
        for e in range(MOE_EPG):
            a = _dot(x, wg_ref[e])
            u = _dot(x, wu_ref[e])
            act_scr[:, e * D_EXPERT:(e + 1) * D_EXPERT] = (_silu(a) * u * comb[:, e:e + 1]).astype(BF16)
        y_ref[...] = _dot(act_scr[...], wd_ref[...].reshape(MOE_EPG * D_EXPERT, D_MODEL))

    @pl.when(i >= n_used_ref[0])
    def _():
        y_ref[...] = jnp.zeros_like(y_ref)


def _experts(tile_group, n_used, xs, w_gate, w_up, w_down):
    n_sorted = xs.shape[0]
    tm = MOE_TM
    w_in_spec = pl.BlockSpec((None, MOE_EPG, D_MODEL, D_EXPERT), lambda i, tg, nu: (tg[i], 0, 0, 0))
    return pl.pallas_call(
        _experts_body,
        grid_spec=pltpu.PrefetchScalarGridSpec(
            num_scalar_prefetch=2,
            grid=(n_sorted // tm,),
            in_specs=[
                pl.BlockSpec((tm, ROW_W), lambda i, tg, nu: (i, 0)),
                w_in_spec,
                w_in_spec,
                pl.BlockSpec((None, MOE_EPG, D_EXPERT, D_MODEL), lambda i, tg, nu: (tg[i], 0, 0, 0)),
            ],
            out_specs=pl.BlockSpec((tm, D_MODEL), lambda i, tg, nu: (i, 0)),
            scratch_shapes=[pltpu.VMEM((tm, MOE_EPG * D_EXPERT), BF16)],
        ),
        out_shape=jax.ShapeDtypeStruct((n_sorted, D_MODEL), F32),
        compiler_params=_cparams("arbitrary"),
        name="moe_experts",
    )(tile_group, n_used, xs, w_gate, w_up, w_down)


def _combine_body(pos_ref, x_ref, y_ref, o_ref, buf, sem):
    base = pl.program_id(0) * DMA_ROWS

    def row_copy(r):
        return pltpu.make_async_copy(y_ref.at[pl.ds(pos_ref[base + r], 1)], buf.at[pl.ds(r, 1)], sem)

    def start(g, carry):
        for k in range(8):
            row_copy(g * 8 + k).start(priority=k % 2)
        return carry

    lax.fori_loop(0, DMA_ROWS // 8, start, 0)
    pltpu.make_async_copy(y_ref.at[pl.ds(0, DMA_ROWS)], buf, sem).wait()
    o_ref[...] = x_ref[...] + buf[...]


def _combine(pos, x, y):
    t, d = x.shape
    return pl.pallas_call(
        _combine_body,
        grid_spec=pltpu.PrefetchScalarGridSpec(
            num_scalar_prefetch=1,
            grid=(t // DMA_ROWS,),
            in_specs=[
                pl.BlockSpec((DMA_ROWS, d), lambda i, pos: (i, 0)),
                pl.BlockSpec(memory_space=pl.ANY),
            ],
            out_specs=pl.BlockSpec((DMA_ROWS, d), lambda i, pos: (i, 0)),
            scratch_shapes=[pltpu.VMEM((DMA_ROWS, d), F32), pltpu.SemaphoreType.DMA(())],
        ),
        out_shape=jax.ShapeDtypeStruct((t, d), F32),
        compiler_params=_cparams("arbitrary"),
        name="moe_combine",
    )(pos, x, y)


def _moe(xf, gain, wg, bg, we, be, w_gate, w_up, w_down):
    t = xf.shape[0]
    w_r = jnp.zeros((D_MODEL, LANES), F32)
    w_r = lax.dynamic_update_slice(w_r, wg.astype(F32), (0, 0))
    w_r = lax.dynamic_update_slice(w_r, we.astype(F32), (0, MOE_GROUPS))
    b_r = _lane_table(((0, bg), (MOE_GROUPS, be)))
    rows, info, cnt = _router(xf, gain, w_r, b_r)

    gidx = info[0].astype(I32)
    rank = info[1].astype(I32)
    counts = cnt[:MOE_GROUPS, 0].astype(I32)
    padded = ((counts + MOE_TM - 1) // MOE_TM) * MOE_TM
    ends = jnp.cumsum(padded)
    starts = ends - padded
    pos = rank + sum(jnp.where(gidx == g, starts[g], 0) for g in range(MOE_GROUPS))
    n_sorted = t + MOE_GROUPS * MOE_TM
    n_tiles = n_sorted // MOE_TM
    tile_start = jnp.arange(n_tiles, dtype=I32) * MOE_TM
    tile_group = jnp.minimum(jnp.sum(tile_start[:, None] >= ends[None, :], axis=1), MOE_GROUPS - 1).astype(I32)
    n_used = (ends[-1:] // MOE_TM).astype(I32)

    pad = jnp.concatenate([starts + counts, padded - counts, n_used]).astype(I32)
    xs = _dispatch(pos, pad, rows, n_sorted)
    ys = _experts(tile_group, n_used, xs, w_gate.astype(BF16), w_up.astype(BF16), w_down.astype(BF16))
    return _combine(pos, xf, ys)


def kernel(x, norm_mix, norm_ffn, even_w_in, mlstm_i_bias, mlstm_f_bias, mlstm_norm, gdn_conv, gdn_A_log, gdn_dt_bias,
           gdn_norm, even_w_out, odd_w_in, att_q_norm, att_k_norm, att_rel_bias, ssd_conv, ssd_conv_bias, ssd_A_log,
           ssd_dt_bias, ssd_D, ssd_norm, odd_w_out, router_group_w, router_group_b, router_expert_w, router_expert_b,
           moe_w_gate, moe_w_up, moe_w_down):
    batch, seq, d = x.shape
    xf = x.reshape(batch * seq, d)
    depth = norm_mix.shape[0]
    for layer in range(depth):
        i = layer // 2
        if layer % 2 == 0:
            xf = _even_mix(xf, batch, seq, norm_mix[layer], even_w_in[i], mlstm_i_bias[i], mlstm_f_bias[i], mlstm_norm[i],
                           gdn_conv[i], gdn_A_log[i], gdn_dt_bias[i], gdn_norm[i], even_w_out[i])
        else:
            xf = _odd_mix(xf, batch, seq, norm_mix[layer], odd_w_in[i], att_q_norm[i], att_k_norm[i], att_rel_bias[i],
                          ssd_conv[i], ssd_conv_bias[i], ssd_A_log[i], ssd_dt_bias[i], ssd_D[i], ssd_norm[i], odd_w_out[i])
        xf = _moe(xf, norm_ffn[layer], router_group_w[layer], router_group_b[layer], router_expert_w[layer],
                  router_expert_b[layer], moe_w_gate[layer], moe_w_up[layer], moe_w_down[layer])
    return xf.reshape(batch, seq, d)
```
